```python
import jax, jax.numpy as jnp
from jax import lax
import numpy as np

D_MODEL = 4096
BATCH = 1
SEQ = 16384
DEPTH = 2
DEC_BATCH = 2
DEC_SEQ = 8192
PAST_LEN = 128

N_META = 16
N_MIXERS = 2
N_CONV_LAYERS = (DEPTH + 1) // 2
N_GLA_LAYERS = DEPTH // 2
CONV_WIDTH = 31
GLA_HEADS = 4
GLA_KEY_DIM = D_MODEL // 2
GLA_VAL_DIM = D_MODEL
GLA_HEAD_K = GLA_KEY_DIM // GLA_HEADS
GLA_HEAD_V = GLA_VAL_DIM // GLA_HEADS
GLA_GATE_RANK = 16
GLA_GATE_TAU = 16.0
GLA_CHUNK = 64
GLA_IN_DIM = 2 * GLA_KEY_DIM + 2 * GLA_VAL_DIM
D_FF = 4 * D_MODEL
EPS = 1e-6

kernel_name = "hybrid_conformer_gla_encoder"


def rms_norm(x, g):
    xf = x.astype(jnp.float32)
    y = xf * lax.rsqrt(jnp.mean(xf * xf, axis=-1, keepdims=True) + EPS)
    return (y * g.astype(jnp.float32)).astype(x.dtype)


def conformer_conv(h, w_pw1, b_pw1, w_dw, b_dw, ln_g, ln_b, w_pw2, b_pw2):
    u = jnp.einsum('btd,de->bte', h, w_pw1) + b_pw1
    a, gate = jnp.split(u, 2, axis=-1)
    u = a * jax.nn.sigmoid(gate)
    half = CONV_WIDTH // 2
    u = lax.conv_general_dilated(
        u, w_dw[:, None, :], window_strides=(1,), padding=[(half, half)],
        dimension_numbers=('NWC', 'WIO', 'NWC'), feature_group_count=D_MODEL) + b_dw
    uf = u.astype(jnp.float32)
    mu = jnp.mean(uf, axis=-1, keepdims=True)
    var = jnp.mean(jnp.square(uf - mu), axis=-1, keepdims=True)
    uf = (uf - mu) * lax.rsqrt(var + EPS) * ln_g.astype(jnp.float32) + ln_b.astype(jnp.float32)
    u = jax.nn.silu(uf).astype(h.dtype)
    return jnp.einsum('btd,de->bte', u, w_pw2) + b_pw2


def gla_chunked(q, k, v, g):
    B, T, H, dk = q.shape
    dv = v.shape[-1]
    nc = T // GLA_CHUNK

    def to_chunks(t):
        return t.reshape(B, nc, GLA_CHUNK, H, t.shape[-1]).transpose(1, 0, 3, 2, 4)

    qc, kc, vc, gc = to_chunks(q), to_chunks(k), to_chunks(v), to_chunks(g)
    mask = jnp.tril(jnp.ones((GLA_CHUNK, GLA_CHUNK), dtype=bool))[:, :, None]

    def step(S, inp):
        qi, ki, vi, gi = inp
        b = jnp.cumsum(gi, axis=2)
        b_last = b[:, :, -1:, :]
        o_inter = jnp.einsum('bhcd,bhde->bhce', qi * jnp.exp(b), S)
        rel = b[:, :, :, None, :] - b[:, :, None, :, :]
        decay = jnp.exp(jnp.where(mask, rel, -jnp.inf))
        scores = jnp.einsum('bhid,bhjd,bhijd->bhij', qi, ki, decay)
        o_intra = jnp.einsum('bhij,bhje->bhie', scores, vi)
        S_new = jnp.exp(b_last[:, :, 0, :])[..., None] * S + jnp.einsum(
            'bhjd,bhje->bhde', ki * jnp.exp(b_last - b), vi)
        return S_new, o_inter + o_intra

    S0 = jnp.zeros((B, H, dk, dv), jnp.float32)
    _, o = lax.scan(step, S0, (qc, kc, vc, gc))
    return o.transpose(1, 0, 3, 2, 4).reshape(B, T, H, dv)


def gla_mixer(h, w_in, w_gate_a, w_gate_b, b_gate, g_norm, w_out):
    B, T, _ = h.shape
    proj = jnp.einsum('btd,de->bte', h, w_in)
    q, k, v, og = jnp.split(proj, [GLA_KEY_DIM, 2 * GLA_KEY_DIM, 2 * GLA_KEY_DIM + GLA_VAL_DIM], axis=-1)
    q = q.reshape(B, T, GLA_HEADS, GLA_HEAD_K).astype(jnp.float32) * (GLA_HEAD_K ** -0.5)
    k = k.reshape(B, T, GLA_HEADS, GLA_HEAD_K).astype(jnp.float32)
    v = v.reshape(B, T, GLA_HEADS, GLA_HEAD_V).astype(jnp.float32)
    lr = jnp.einsum('btd,zdr->zbtr', h, w_gate_a)
    pre = jnp.einsum('zbtr,zrk->zbtk', lr, w_gate_b).astype(jnp.float32) + b_gate[:, None, None, :].astype(jnp.float32)
    glog = (jax.nn.log_sigmoid(pre) / GLA_GATE_TAU).reshape(2, B, T, GLA_HEADS, GLA_HEAD_K)

    pad = (-T) % GLA_CHUNK

    def run(qq, kk, vv, gg, front):
        cfg = ((0, 0), (pad, 0), (0, 0), (0, 0)) if front else ((0, 0), (0, pad), (0, 0), (0, 0))
        o = gla_chunked(jnp.pad(qq, cfg), jnp.pad(kk, cfg), jnp.pad(vv, cfg), jnp.pad(gg, cfg))
        return o[:, pad:] if front else o[:, :T]

    def flip(t):
        return t[:, ::-1]

    o_fwd = run(q, k, v, glog[0], True)
    o_bwd = flip(run(flip(q), flip(k), flip(v), flip(glog[1]), False))
    o = o_fwd + o_bwd
    o = o * lax.rsqrt(jnp.mean(o * o, axis=-1, keepdims=True) + EPS)
    o = o.reshape(B, T, GLA_VAL_DIM) * g_norm.astype(jnp.float32)
    o = (o * jax.nn.silu(og.astype(jnp.float32))).astype(h.dtype)
    return jnp.einsum('bte,ed->btd', o, w_out)


def squared_relu_mlp(h, w1, w2):
    u = jnp.einsum('btd,df->btf', h, w1)
    u = jnp.square(jax.nn.relu(u))
    return jnp.einsum('btf,fd->btd', u, w2)


def trunk(x, meta_tokens, norm_mix, norm_mlp, norm_final,
          conv_w_pw1, conv_b_pw1, conv_w_dw, conv_b_dw, conv_ln_g, conv_ln_b, conv_w_pw2, conv_b_pw2,
          gla_w_in, gla_w_gate_a, gla_w_gate_b, gla_b_gate, gla_norm_g, gla_w_out,
          mlp_w1, mlp_w2):
    B = x.shape[0]
    meta = jnp.broadcast_to(meta_tokens.astype(x.dtype)[None], (B, N_META, D_MODEL))
    h = jnp.concatenate([meta, x], axis=1)
    for i in range(DEPTH):
        z = rms_norm(h, norm_mix[i])
        j = i // N_MIXERS
        if i % N_MIXERS == 0:
            h = h + conformer_conv(z, conv_w_pw1[j], conv_b_pw1[j], conv_w_dw[j], conv_b_dw[j],
                                   conv_ln_g[j], conv_ln_b[j], conv_w_pw2[j], conv_b_pw2[j])
        else:
            h = h + gla_mixer(z, gla_w_in[j], gla_w_gate_a[j], gla_w_gate_b[j], gla_b_gate[j],
                              gla_norm_g[j], gla_w_out[j])
        z = rms_norm(h, norm_mlp[i])
        h = h + squared_relu_mlp(z, mlp_w1[i], mlp_w2[i])
    return rms_norm(h, norm_final)[:, N_META:]


def setup_inputs(seed: int = 0) -> dict:
    key = jax.random.key(seed)
    ks = jax.random.split(key, 24)
    f32 = jnp.float32
    D = D_MODEL

    def nrm(k, shape, scale):
        return jax.random.normal(k, shape, f32) * scale

    return {
        "x_prompt": nrm(ks[0], (BATCH, SEQ, D), 1.0),
        "x_sample": nrm(ks[1], (DEC_BATCH, DEC_SEQ, D), 1.0),
        "meta_tokens": nrm(ks[2], (N_META, D), 1.0),
        "norm_mix": 1.0 + nrm(ks[3], (DEPTH, D), 0.02),
        "norm_mlp": 1.0 + nrm(ks[4], (DEPTH, D), 0.02),
        "norm_final": 1.0 + nrm(ks[5], (D,), 0.02),
        "conv_w_pw1": nrm(ks[6], (N_CONV_LAYERS, D, 2 * D), D ** -0.5),
        "conv_b_pw1": nrm(ks[7], (N_CONV_LAYERS, 2 * D), 0.02),
        "conv_w_dw": nrm(ks[8], (N_CONV_LAYERS, CONV_WIDTH, D), CONV_WIDTH ** -0.5),
        "conv_b_dw": nrm(ks[9], (N_CONV_LAYERS, D), 0.02),
        "conv_ln_g": 1.0 + nrm(ks[10], (N_CONV_LAYERS, D), 0.02),
        "conv_ln_b": nrm(ks[11], (N_CONV_LAYERS, D), 0.02),
        "conv_w_pw2": nrm(ks[12], (N_CONV_LAYERS, D, D), D ** -0.5),
        "conv_b_pw2": nrm(ks[13], (N_CONV_LAYERS, D), 0.02),
        "gla_w_in": nrm(ks[14], (N_GLA_LAYERS, D, GLA_IN_DIM), D ** -0.5),
        "gla_w_gate_a": nrm(ks[15], (N_GLA_LAYERS, 2, D, GLA_GATE_RANK), D ** -0.5),
        "gla_w_gate_b": nrm(ks[16], (N_GLA_LAYERS, 2, GLA_GATE_RANK, GLA_KEY_DIM), GLA_GATE_RANK ** -0.5),
        "gla_b_gate": 3.0 + nrm(ks[17], (N_GLA_LAYERS, 2, GLA_KEY_DIM), 0.5),
        "gla_norm_g": 1.0 + nrm(ks[18], (N_GLA_LAYERS, GLA_VAL_DIM), 0.02),
        "gla_w_out": nrm(ks[19], (N_GLA_LAYERS, GLA_VAL_DIM, D), GLA_VAL_DIM ** -0.5),
        "mlp_w1": nrm(ks[20], (DEPTH, D, D_FF), D ** -0.5),
        "mlp_w2": nrm(ks[21], (DEPTH, D_FF, D), D_FF ** -0.5),
    }


def reference(x_prompt, x_sample, meta_tokens, norm_mix, norm_mlp, norm_final,
              conv_w_pw1, conv_b_pw1, conv_w_dw, conv_b_dw, conv_ln_g, conv_ln_b, conv_w_pw2, conv_b_pw2,
              gla_w_in, gla_w_gate_a, gla_w_gate_b, gla_b_gate, gla_norm_g, gla_w_out,
              mlp_w1, mlp_w2):
    y_prompt = trunk(x_prompt, meta_tokens, norm_mix, norm_mlp, norm_final,
                     conv_w_pw1, conv_b_pw1, conv_w_dw, conv_b_dw, conv_ln_g, conv_ln_b, conv_w_pw2, conv_b_pw2,
                     gla_w_in, gla_w_gate_a, gla_w_gate_b, gla_b_gate, gla_norm_g, gla_w_out,
                     mlp_w1, mlp_w2)
    y_sample = trunk(x_sample, meta_tokens, norm_mix, norm_mlp, norm_final,
                     conv_w_pw1, conv_b_pw1, conv_w_dw, conv_b_dw, conv_ln_g, conv_ln_b, conv_w_pw2, conv_b_pw2,
                     gla_w_in, gla_w_gate_a, gla_w_gate_b, gla_b_gate, gla_norm_g, gla_w_out,
                     mlp_w1, mlp_w2)
    return (y_prompt, y_sample)
```

```python
import functools

import jax
import jax.numpy as jnp
from jax import lax
from jax.experimental import pallas as pl
from jax.experimental.pallas import tpu as pltpu

N_META = 16
GLA_HEADS = 4
GLA_GATE_TAU = 16.0
GLA_CHUNK = 64
GLA_SUB = 16
EPS = 1e-6
HALO = 16
LANES = 128
VMEM_LIMIT = 52 * 1024 * 1024

F32 = jnp.float32
BF16 = jnp.bfloat16


def _cparams(sem):
    return pltpu.CompilerParams(dimension_semantics=sem, vmem_limit_bytes=VMEM_LIMIT)


def _rmsnorm_kernel(x_ref, g_ref, o_ref):
    x = x_ref[...].astype(F32)
    ms = jnp.mean(x * x, axis=-1, keepdims=True)
    o_ref[...] = (x * lax.rsqrt(ms + EPS) * g_ref[...]).astype(o_ref.dtype)


def _rmsnorm(x, g, out_dtype):
    m, d = x.shape
    tr = min(256, m)
    return pl.pallas_call(
        _rmsnorm_kernel,
        grid=(m // tr,),
        in_specs=[pl.BlockSpec((tr, d), lambda i: (i, 0)),
                  pl.BlockSpec((1, d), lambda i: (0, 0))],
        out_specs=pl.BlockSpec((tr, d), lambda i: (i, 0)),
        out_shape=jax.ShapeDtypeStruct((m, d), out_dtype),
        compiler_params=_cparams(("parallel",)),
        name="rmsnorm",
    )(x, g.reshape(1, d).astype(F32))


def _accumulate(acc_ref, prod):
    k = pl.program_id(2)

    @pl.when(k == 0)
    def _():
        acc_ref[...] = prod

    @pl.when(k > 0)
    def _():
        acc_ref[...] += prod


def _mm_kernel(*refs, nk, kind, has_bias):
    x_ref, w_ref = refs[0], refs[1]
    pos = 2
    b_ref = r_ref = None
    if has_bias:
        b_ref = refs[pos]
        pos += 1
    if kind == "residual":
        r_ref = refs[pos]
        pos += 1
    o_ref, acc_ref = refs[pos], refs[pos + 1]

    _accumulate(acc_ref, jnp.dot(x_ref[...], w_ref[...], preferred_element_type=F32))

    @pl.when(pl.program_id(2) == nk - 1)
    def _():
        y = acc_ref[...]
        if has_bias:
            y = y + b_ref[...]
        if kind == "relu2":
            y = jnp.square(jnp.maximum(y, 0.0))
        elif kind == "residual":
            y = r_ref[...] + y
        o_ref[...] = y.astype(o_ref.dtype)


def _mm_tiles(m, n, k):
    return min(1024, m), min(1024, n), min(2048, k)


def _matmul(x, w, *, kind="plain", bias=None, residual=None, out_dtype=BF16):
    m, kdim = x.shape
    n = w.shape[1]
    tm, tn, tk = _mm_tiles(m, n, kdim)
    nk = kdim // tk
    in_specs = [pl.BlockSpec((tm, tk), lambda i, j, k: (i, k)),
                pl.BlockSpec((tk, tn), lambda i, j, k: (k, j))]
    args = [x, w]
    if bias is not None:
        in_specs.append(pl.BlockSpec((1, tn), lambda i, j, k: (0, j)))
        args.append(bias.reshape(1, n).astype(F32))
    if kind == "residual":
        in_specs.append(pl.BlockSpec((tm, tn), lambda i, j, k: (i, j)))
        args.append(residual)
    return pl.pallas_call(
        functools.partial(_mm_kernel, nk=nk, kind=kind, has_bias=bias is not None),
        grid=(m // tm, n // tn, nk),
        in_specs=in_specs,
        out_specs=pl.BlockSpec((tm, tn), lambda i, j, k: (i, j)),
        out_shape=jax.ShapeDtypeStruct((m, n), out_dtype),
        scratch_shapes=[pltpu.VMEM((tm, tn), F32)],
        compiler_params=_cparams(("parallel", "parallel", "arbitrary")),
        name="matmul_" + kind,
    )(*args)


def _glu_kernel(x_ref, wa_ref, wg_ref, ba_ref, bg_ref, o_ref, acca_ref, accg_ref, *, nk):
    x = x_ref[...]
    _accumulate(acca_ref, jnp.dot(x, wa_ref[...], preferred_element_type=F32))
    _accumulate(accg_ref, jnp.dot(x, wg_ref[...], preferred_element_type=F32))

    @pl.when(pl.program_id(2) == nk - 1)
    def _():
        a = acca_ref[...] + ba_ref[...]
        g = accg_ref[...] + bg_ref[...]
        o_ref[...] = (a * jax.nn.sigmoid(g)).astype(o_ref.dtype)


def _matmul_glu(x, w, bias):
    m, kdim = x.shape
    n = w.shape[1] // 2
    tm, tn, tk = _mm_tiles(m, n, kdim)
    nk = kdim // tk
    nj = n // tn
    b = bias.reshape(1, 2 * n).astype(F32)
    return pl.pallas_call(
        functools.partial(_glu_kernel, nk=nk),
        grid=(m // tm, nj, nk),
        in_specs=[pl.BlockSpec((tm, tk), lambda i, j, k: (i, k)),
                  pl.BlockSpec((tk, tn), lambda i, j, k: (k, j)),
                  pl.BlockSpec((tk, tn), lambda i, j, k: (k, j + nj)),
                  pl.BlockSpec((1, tn), lambda i, j, k: (0, j)),
                  pl.BlockSpec((1, tn), lambda i, j, k: (0, j + nj))],
        out_specs=pl.BlockSpec((tm, tn), lambda i, j, k: (i, j)),
        out_shape=jax.ShapeDtypeStruct((m, n), BF16),
        scratch_shapes=[pltpu.VMEM((tm, tn), F32), pltpu.VMEM((tm, tn), F32)],
        compiler_params=_cparams(("parallel", "parallel", "arbitrary")),
        name="matmul_glu",
    )(x, w, w, b, b)


def _conv_kernel(cur_ref, prev_ref, next_ref, w_ref, b_ref, g_ref, beta_ref, o_ref,
                 ext_ref, conv_ref, *, width):
    tt, d = cur_ref.shape
    ext_ref[0:HALO, :] = prev_ref[0].astype(F32)
    ext_ref[HALO:HALO + tt, :] = cur_ref[...].astype(F32)
    ext_ref[HALO + tt:HALO + tt + HALO, :] = next_ref[0].astype(F32)
    shift = HALO - width // 2

    def col_body(cb, carry):
        c0 = pl.multiple_of(cb * LANES, LANES)
        acc = jnp.zeros((tt, LANES), F32)
        for j in range(width):
            tap = ext_ref[pl.ds(j + shift, tt), pl.ds(c0, LANES)]
            acc = acc + tap * w_ref[j:j + 1, pl.ds(c0, LANES)]
        conv_ref[:, pl.ds(c0, LANES)] = acc + b_ref[:, pl.ds(c0, LANES)]
        return carry

    lax.fori_loop(0, d // LANES, col_body, 0)

    u = conv_ref[...]
    mu = jnp.mean(u, axis=-1, keepdims=True)
    uc = u - mu
    var = jnp.mean(uc * uc, axis=-1, keepdims=True)
    y = uc * lax.rsqrt(var + EPS) * g_ref[...] + beta_ref[...]
    o_ref[...] = (y * jax.nn.sigmoid(y)).astype(o_ref.dtype)


def _conv_ln_swish(cur, prev, nxt, w_dw, b_dw, ln_g, ln_b, tt):
    m, d = cur.shape
    width = w_dw.shape[0]
    vec = lambda v: v.reshape(1, d).astype(F32)
    return pl.pallas_call(
        functools.partial(_conv_kernel, width=width),
        grid=(m // tt,),
        in_specs=[pl.BlockSpec((tt, d), lambda i: (i, 0)),
                  pl.BlockSpec((1, HALO, d), lambda i: (i, 0, 0)),
                  pl.BlockSpec((1, HALO, d), lambda i: (i, 0, 0)),
                  pl.BlockSpec((width, d), lambda i: (0, 0)),
                  pl.BlockSpec((1, d), lambda i: (0, 0)),
                  pl.BlockSpec((1, d), lambda i: (0, 0)),
                  pl.BlockSpec((1, d), lambda i: (0, 0))],
        out_specs=pl.BlockSpec((tt, d), lambda i: (i, 0)),
        out_shape=jax.ShapeDtypeStruct((m, d), BF16),
        scratch_shapes=[pltpu.VMEM((tt + 2 * HALO, d), F32), pltpu.VMEM((tt, d), F32)],
        compiler_params=_cparams(("parallel",)),
        name="conv_ln_swish",
    )(cur, prev, nxt, w_dw.astype(F32), vec(b_dw), vec(ln_g), vec(ln_b))


def _dot_nt(a, b):
    return lax.dot_general(a, b, (((1,), (1,)), ((), ())), preferred_element_type=F32)


def _dot_tn(a, b):
    return lax.dot_general(a, b, (((0,), (0,)), ((), ())), preferred_element_type=F32)


def _gla_chunk(q, k, v, g, st_ref, *, rev):
    c, dk = q.shape
    row = lax.broadcasted_iota(jnp.int32, (c, c), 0)
    col = lax.broadcasted_iota(jnp.int32, (c, c), 1)
    tri = ((col >= row) if rev else (col <= row)).astype(BF16)
    g_hi = g.astype(BF16)
    r1 = g - g_hi.astype(F32)
    g_mid = r1.astype(BF16)
    g_lo = (r1 - g_mid.astype(F32)).astype(BF16)
    cum = (jnp.dot(tri, g_hi, preferred_element_type=F32)
           + jnp.dot(tri, g_mid, preferred_element_type=F32)
           + jnp.dot(tri, g_lo, preferred_element_type=F32))
    tot = cum[0:1] if rev else cum[c - 1:c]

    st = st_ref[...]
    o = _dot_nt((q * jnp.exp(cum)).astype(BF16), st.astype(BF16))

    nsb = c // GLA_SUB
    rowl = lax.broadcasted_iota(jnp.int32, (GLA_SUB, c), 0)
    coll = lax.broadcasted_iota(jnp.int32, (GLA_SUB, c), 1)
    blocks = []
    for blk in range(nsb):
        r0 = blk * GLA_SUB
        q_b = q[r0:r0 + GLA_SUB]
        cum_b = cum[r0:r0 + GLA_SUB]
        has_off = (blk < nsb - 1) if rev else (blk > 0)
        if has_off:
            if rev:
                ref = cum[r0 + GLA_SUB:r0 + GLA_SUB + 1]
                k_o, cum_o = k[r0 + GLA_SUB:], cum[r0 + GLA_SUB:]
            else:
                ref = cum[r0 - 1:r0]
                k_o, cum_o = k[:r0], cum[:r0]
            qn = (q_b * jnp.exp(jnp.minimum(cum_b - ref, 0.0))).astype(BF16)
            kn = (k_o * jnp.exp(jnp.minimum(ref - cum_o, 0.0))).astype(BF16)
            pad = jnp.zeros((c - kn.shape[0], dk), BF16)
            kn = jnp.concatenate([pad, kn] if rev else [kn, pad], axis=0)
            a = _dot_nt(qn, kn)
        else:
            a = jnp.zeros((GLA_SUB, c), F32)
        for jj in range(GLA_SUB):
            j = r0 + jj
            dec = jnp.exp(jnp.minimum(cum_b - cum[j:j + 1], 0.0))
            s = jnp.sum(q_b * dec * k[j:j + 1], axis=-1, keepdims=True)
            seen = (rowl <= jj) if rev else (rowl >= jj)
            a = jnp.where((coll == j) & seen, s, a)
        blocks.append(a)
    scores = jnp.concatenate(blocks, axis=0).astype(BF16)
    o = o + jnp.dot(scores, v, preferred_element_type=F32)

    ke = (k * jnp.exp(tot - cum)).astype(BF16)
    st_ref[...] = st * jnp.exp(tot) + _dot_tn(v, ke)
    return o


def _gla_kernel(*refs, rev, nc, final, scale):
    (qm, km, vm, lrm, qx, kx, vx, lrx, wb_ref, bg_ref) = refs[:10]
    pos = 10
    if final:
        obm, obx, ogm, ogx, gn_ref = refs[pos:pos + 5]
        pos += 5
    om, ox, st_ref = refs[pos], refs[pos + 1], refs[pos + 2]

    s = pl.program_id(2)
    chunk = (nc - 1 - s) if rev else s
    is_meta = chunk == 0

    @pl.when(s == 0)
    def _():
        st_ref[...] = jnp.zeros_like(st_ref)

    pick = lambda x_ref, m_ref: jnp.where(is_meta, x_ref[...], m_ref[...])
    q = pick(qx, qm).astype(F32) * scale
    k = pick(kx, km).astype(F32)
    v = pick(vx, vm)
    lr = pick(lrx, lrm)
    c = q.shape[0]

    pre = jnp.dot(lr, wb_ref[...], preferred_element_type=F32) + bg_ref[...]
    g = (jnp.minimum(pre, 0.0) - jnp.log(1.0 + jnp.exp(-jnp.abs(pre)))) / GLA_GATE_TAU
    rows = lax.broadcasted_iota(jnp.int32, g.shape, 0)
    g = jnp.where(is_meta & (rows < c - N_META), 0.0, g)

    o = _gla_chunk(q, k, v, g, st_ref, rev=rev)

    if final:
        o = o + pick(obx, obm)
        ms = jnp.mean(o * o, axis=-1, keepdims=True)
        og = pick(ogx, ogm).astype(F32)
        o = o * lax.rsqrt(ms + EPS) * gn_ref[...] * (og * jax.nn.sigmoid(og))

    @pl.when(is_meta)
    def _():
        ox[...] = o.astype(ox.dtype)

    @pl.when(jnp.logical_not(is_meta))
    def _():
        om[...] = o.astype(om.dtype)


def _gla_pass(proj_m, proj_x, lr_m, lr_x, wb_pad, b_gate, *, batch, rev, final=None):
    rows, width = proj_m.shape
    key_dim = wb_pad.shape[1]
    dk = key_dim // GLA_HEADS
    dv = (width - 2 * key_dim) // 2 // GLA_HEADS
    c = GLA_CHUNK
    nmain = rows // batch // c
    nc = nmain + 1
    h = GLA_HEADS

    def chunk_of(s):
        return (nc - 1 - s) if rev else s

    def main_row(b, s):
        return b * nmain + jnp.maximum(chunk_of(s) - 1, 0)

    def main_spec(w, col0):
        return pl.BlockSpec((c, w), lambda b, hh, s: (main_row(b, s), col0 + hh))

    def meta_spec(w, col0):
        return pl.BlockSpec((c, w), lambda b, hh, s: (b, col0 + hh))

    in_specs = [main_spec(dk, 0), main_spec(dk, h), main_spec(dv, h),
                pl.BlockSpec((c, LANES), lambda b, hh, s: (main_row(b, s), 0)),
                meta_spec(dk, 0), meta_spec(dk, h), meta_spec(dv, h),
                pl.BlockSpec((c, LANES), lambda b, hh, s: (b, 0)),
                pl.BlockSpec((LANES, dk), lambda b, hh, s: (0, hh)),
                pl.BlockSpec((1, dk), lambda b, hh, s: (0, hh))]
    args = [proj_m, proj_m, proj_m, lr_m, proj_x, proj_x, proj_x, lr_x,
            wb_pad, b_gate.reshape(1, key_dim).astype(F32)]
    out_dtype = F32
    if final is not None:
        ob_m, ob_x, g_norm = final
        in_specs += [main_spec(dv, 0), meta_spec(dv, 0), main_spec(dv, 2 * h), meta_spec(dv, 2 * h),
                     pl.BlockSpec((1, dv), lambda b, hh, s: (0, hh))]
        args += [ob_m, ob_x, proj_m, proj_x, g_norm.reshape(1, h * dv).astype(F32)]
        out_dtype = BF16
    return pl.pallas_call(
        functools.partial(_gla_kernel, rev=rev, nc=nc, final=final is not None, scale=dk ** -0.5),
        grid=(batch, h, nc),
        in_specs=in_specs,
        out_specs=[main_spec(dv, 0), meta_spec(dv, 0)],
        out_shape=[jax.ShapeDtypeStruct((rows, h * dv), out_dtype),
                   jax.ShapeDtypeStruct((batch * c, h * dv), out_dtype)],
        scratch_shapes=[pltpu.VMEM((dv, dk), F32)],
        compiler_params=_cparams(("parallel", "parallel", "arbitrary")),
        name="gla_rev" if rev else "gla_fwd",
    )(*args)


def _pad_meta_rows(x, batch):
    w = x.shape[1]
    x = x.reshape(batch, N_META, w)
    x = jnp.pad(x, ((0, 0), (GLA_CHUNK - N_META, 0), (0, 0)))
    return x.reshape(batch * GLA_CHUNK, w)


def _trunk(x, p):
    batch, seq, d = x.shape
    hm = x.reshape(batch * seq, d)
    hx = jnp.broadcast_to(p["meta_tokens"][None], (batch, N_META, d)).reshape(batch * N_META, d)

    zm = _rmsnorm(hm, p["norm_mix"][0], BF16)
    zx = _rmsnorm(hx, p["norm_mix"][0], BF16)
    um = _matmul_glu(zm, p["conv_w_pw1"], p["conv_b_pw1"])
    ux = _matmul_glu(zx, p["conv_w_pw1"], p["conv_b_pw1"])

    tt = min(256, seq)
    nt = seq // tt
    um4 = um.reshape(batch, nt, tt, d)
    tails = um4[:, :, tt - HALO:, :]
    heads = um4[:, :, :HALO, :]
    ux3 = ux.reshape(batch, 1, N_META, d)
    zeros = jnp.zeros((batch, 1, HALO, d), BF16)
    prev_m = jnp.concatenate([ux3, tails[:, :-1]], axis=1).reshape(batch * nt, HALO, d)
    next_m = jnp.concatenate([heads[:, 1:], zeros], axis=1).reshape(batch * nt, HALO, d)
    conv_args = (p["conv_w_dw"], p["conv_b_dw"], p["conv_ln_g"], p["conv_ln_b"])
    cm = _conv_ln_swish(um, prev_m, next_m, *conv_args, tt)
    cx = _conv_ln_swish(ux, zeros.reshape(batch, HALO, d), heads[:, 0], *conv_args, N_META)

    hm = _matmul(cm, p["conv_w_pw2"], kind="residual", bias=p["conv_b_pw2"], residual=hm, out_dtype=F32)
    hx = _matmul(cx, p["conv_w_pw2"], kind="residual", bias=p["conv_b_pw2"], residual=hx, out_dtype=F32)

    def mlp(h, i):
        z = _rmsnorm(h, p["norm_mlp"][i], BF16)
        u = _matmul(z, p["mlp_w1"][i], kind="relu2")
        return _matmul(u, p["mlp_w2"][i], kind="residual", residual=h, out_dtype=F32)

    hm = mlp(hm, 0)
    hx = mlp(hx, 0)

    zm = _rmsnorm(hm, p["norm_mix"][1], BF16)
    zx = _rmsnorm(hx, p["norm_mix"][1], BF16)
    proj_m = _matmul(zm, p["gla_w_in"])
    proj_x = _pad_meta_rows(_matmul(zx, p["gla_w_in"]), batch)
    lr_m = _matmul(zm, p["gla_wa"], out_dtype=F32)
    lr_x = _pad_meta_rows(_matmul(zx, p["gla_wa"], out_dtype=F32), batch)

    ob_m, ob_x = _gla_pass(proj_m, proj_x, lr_m, lr_x, p["gla_wb"][1], p["gla_b_gate"][1],
                           batch=batch, rev=True)
    gm, _ = _gla_pass(proj_m, proj_x, lr_m, lr_x, p["gla_wb"][0], p["gla_b_gate"][0],
                      batch=batch, rev=False, final=(ob_m, ob_x, p["gla_norm_g"]))
    hm = _matmul(gm, p["gla_w_out"], kind="residual", residual=hm, out_dtype=F32)
    hm = mlp(hm, 1)

    y = _rmsnorm(hm, p["norm_final"], F32)
    return y.reshape(batch, seq, d)


def kernel(x_prompt, x_sample, meta_tokens, norm_mix, norm_mlp, norm_final, conv_w_pw1, conv_b_pw1, conv_w_dw, conv_b_dw, conv_ln_g, conv_ln_b, conv_w_pw2, conv_b_pw2, gla_w_in, gla_w_gate_a, gla_w_gate_b, gla_b_gate, gla_norm_g, gla_w_out, mlp_w1, mlp_w2):
    d = x_prompt.shape[-1]
    rank = gla_w_gate_a.shape[-1]
    key_dim = gla_w_gate_b.shape[-1]
    wa = jnp.concatenate([gla_w_gate_a[0, 0], gla_w_gate_a[0, 1],
                          jnp.zeros((d, LANES - 2 * rank), F32)], axis=1).astype(BF16)
    wb = jnp.stack([
        jnp.pad(gla_w_gate_b[0, z], ((z * rank, LANES - (z + 1) * rank), (0, 0))) for z in range(2)])
    p = dict(
        meta_tokens=meta_tokens, norm_mix=norm_mix, norm_mlp=norm_mlp, norm_final=norm_final,
        conv_w_pw1=conv_w_pw1[0].astype(BF16), conv_b_pw1=conv_b_pw1[0],
        conv_w_dw=conv_w_dw[0], conv_b_dw=conv_b_dw[0], conv_ln_g=conv_ln_g[0], conv_ln_b=conv_ln_b[0],
        conv_w_pw2=conv_w_pw2[0].astype(BF16), conv_b_pw2=conv_b_pw2[0],
        gla_w_in=gla_w_in[0].astype(BF16), gla_wa=wa, gla_wb=wb, gla_b_gate=gla_b_gate[0],
        gla_norm_g=gla_norm_g[0], gla_w_out=gla_w_out[0].astype(BF16),
        mlp_w1=mlp_w1.astype(BF16), mlp_w2=mlp_w2.astype(BF16),
    )
    return (_trunk(x_prompt, p), _trunk(x_sample, p))
```

```python
import functools

import jax
import jax.numpy as jnp
from jax import lax
from jax.experimental import pallas as pl
from jax.experimental.pallas import tpu as pltpu

N_META = 16
GLA_HEADS = 4
GLA_HEADS_PER_STEP = 2
GLA_GATE_TAU = 16.0
GLA_CHUNK = 64
GLA_SUB = 16
EPS = 1e-6
LOG2_E = 1.4426950408889634
HALO = 16
SUBLANES = 8
LANES = 128
VMEM_LIMIT = 52 * 1024 * 1024
MAX_FULL_K = 4096

F32 = jnp.float32
BF16 = jnp.bfloat16


def _cparams(sem):
    return pltpu.CompilerParams(dimension_semantics=sem, vmem_limit_bytes=VMEM_LIMIT)


def _rmsnorm_kernel(x_ref, g_ref, o_ref):
    x = x_ref[...].astype(F32)
    ms = jnp.mean(x * x, axis=-1, keepdims=True)
    o_ref[...] = (x * lax.rsqrt(ms + EPS) * g_ref[...]).astype(o_ref.dtype)


def _rmsnorm(x, g, out_dtype):
    m, d = x.shape
    tr = min(256, m)
    return pl.pallas_call(
        _rmsnorm_kernel,
        grid=(m // tr,),
        in_specs=[pl.BlockSpec((tr, d), lambda i: (i, 0)),
                  pl.BlockSpec((1, d), lambda i: (0, 0))],
        out_specs=pl.BlockSpec((tr, d), lambda i: (i, 0)),
        out_shape=jax.ShapeDtypeStruct((m, d), out_dtype),
        compiler_params=_cparams(("parallel",)),
        name="rmsnorm",
    )(x, g.reshape(1, d).astype(F32))


def _mm_tiles(m, n, k, wide):
    if k > MAX_FULL_K:
        return min(1024, m), min(1024, n), MAX_FULL_K // 2
    return min(1024, m), min(1024 if wide else 512, n), k


def _mm_kernel(*refs, nk, kind, has_bias):
    x_ref, w_ref = refs[0], refs[1]
    pos = 2
    b_ref = r_ref = None
    if has_bias:
        b_ref = refs[pos]
        pos += 1
    if kind == "residual":
        r_ref = refs[pos]
        pos += 1
    o_ref = refs[pos]

    def prod():
        return jnp.dot(x_ref[...], w_ref[...], preferred_element_type=F32)

    if kind == "residual":
        def base():
            return r_ref[...] + b_ref[...] if has_bias else r_ref[...]

        if nk == 1:
            o_ref[...] = base() + prod()
        else:
            @pl.when(pl.program_id(2) == 0)
            def _():
                o_ref[...] = base()

            o_ref[...] += prod()
        return

    assert nk == 1
    y = prod() + b_ref[...] if has_bias else prod()
    if kind == "relu2":
        y = jnp.square(jnp.maximum(y, 0.0))
    o_ref[...] = y.astype(o_ref.dtype)


def _matmul(x, w, *, kind="plain", bias=None, residual=None, out_dtype=BF16):
    m, kdim = x.shape
    n = w.shape[1]
    tm, tn, tk = _mm_tiles(m, n, kdim, wide=kind != "residual")
    nk = kdim // tk
    in_specs = [pl.BlockSpec((tm, tk), lambda i, j, k: (i, k)),
                pl.BlockSpec((tk, tn), lambda i, j, k: (k, j))]
    args = [x, w]
    if bias is not None:
        in_specs.append(pl.BlockSpec((1, tn), lambda i, j, k: (0, j)))
        args.append(bias.reshape(1, n).astype(F32))
    if kind == "residual":
        in_specs.append(pl.BlockSpec((tm, tn), lambda i, j, k: (i, j)))
        args.append(residual)
    return pl.pallas_call(
        functools.partial(_mm_kernel, nk=nk, kind=kind, has_bias=bias is not None),
        grid=(m // tm, n // tn, nk),
        in_specs=in_specs,
        out_specs=pl.BlockSpec((tm, tn), lambda i, j, k: (i, j)),
        out_shape=jax.ShapeDtypeStruct((m, n), out_dtype),
        compiler_params=_cparams(("parallel", "parallel", "arbitrary")),
        name="matmul_" + kind,
    )(*args)


def _glu_kernel(x_ref, wa_ref, wg_ref, ba_ref, bg_ref, o_ref):
    x = x_ref[...]
    a = jnp.dot(x, wa_ref[...], preferred_element_type=F32) + ba_ref[...]
    g = jnp.dot(x, wg_ref[...], preferred_element_type=F32) + bg_ref[...]
    o_ref[...] = (a * jax.nn.sigmoid(g)).astype(o_ref.dtype)


def _matmul_glu(x, w, bias):
    m, kdim = x.shape
    n = w.shape[1] // 2
    tm, tn, tk = _mm_tiles(m, n, kdim, wide=False)
    assert tk == kdim
    nj = n // tn
    b = bias.reshape(1, 2 * n).astype(F32)
    return pl.pallas_call(
        _glu_kernel,
        grid=(m // tm, nj),
        in_specs=[pl.BlockSpec((tm, tk), lambda i, j: (i, 0)),
                  pl.BlockSpec((tk, tn), lambda i, j: (0, j)),
                  pl.BlockSpec((tk, tn), lambda i, j: (0, j + nj)),
                  pl.BlockSpec((1, tn), lambda i, j: (0, j)),
                  pl.BlockSpec((1, tn), lambda i, j: (0, j + nj))],
        out_specs=pl.BlockSpec((tm, tn), lambda i, j: (i, j)),
        out_shape=jax.ShapeDtypeStruct((m, n), BF16),
        compiler_params=_cparams(("parallel", "parallel")),
        name="matmul_glu",
    )(x, w, w, b, b)


def _row_strides(tt):
    n = tt // SUBLANES
    if n % SUBLANES:
        return [n]
    a = n // 2 + 1
    return [a, n - a]


def _conv_kernel(cur_ref, prev_ref, next_ref, meta_ref, w_ref, b_ref, g_ref, beta_ref, o_ref,
                 ext_ref, conv_ref, *, width, tiles_per_seq, meta_mode):
    tt, d = cur_ref.shape
    nslab = d // LANES
    i = pl.program_id(0)
    shift = HALO - width // 2
    strides = _row_strides(tt)
    if not meta_mode:
        first = i % tiles_per_seq == 0
        last = i % tiles_per_seq == tiles_per_seq - 1

    def col_body(cb, carry):
        lanes = pl.ds(pl.multiple_of(cb * LANES, LANES), LANES)
        if meta_mode:
            before = jnp.zeros((HALO, LANES), F32)
            after = next_ref[:, lanes].astype(F32)
        else:
            before = jnp.where(first, meta_ref[:, lanes], prev_ref[:, lanes]).astype(F32)
            after = jnp.where(last, 0.0, next_ref[:, lanes].astype(F32))
        ext_ref[0:HALO, :] = before
        ext_ref[HALO:HALO + tt, :] = cur_ref[:, lanes].astype(F32)
        ext_ref[HALO + tt:HALO + tt + HALO, :] = after
        taps_w = [jnp.broadcast_to(w_ref[cb, j:j + 1, :], (SUBLANES, LANES)) for j in range(width)]
        bias = b_ref[cb]
        base = 0
        for s in strides:
            for r in range(s):
                acc = None
                for j in range(width):
                    tap = ext_ref[pl.ds(base + r + j + shift, SUBLANES, stride=s), :]
                    acc = tap * taps_w[j] if acc is None else acc + tap * taps_w[j]
                conv_ref[cb, pl.ds(base + r, SUBLANES, stride=s), :] = acc + bias
            base += SUBLANES * s
        return carry

    lax.fori_loop(0, nslab, col_body, 0)

    rows_per_group = 2 * SUBLANES

    def row_body(gi, carry):
        rows = pl.ds(pl.multiple_of(gi * rows_per_group, rows_per_group), rows_per_group)
        u = conv_ref[:, rows, :]
        mu = jnp.sum(jnp.sum(u, axis=0), axis=-1, keepdims=True) / d
        uc = u - mu
        var = jnp.sum(jnp.sum(uc * uc, axis=0), axis=-1, keepdims=True) / d
        y = uc * lax.rsqrt(var + EPS) * g_ref[...] + beta_ref[...]
        y = y * jax.nn.sigmoid(y)
        o_ref[rows, :] = jnp.concatenate([y[c] for c in range(nslab)], axis=-1).astype(o_ref.dtype)
        return carry

    ngroups = tt // rows_per_group
    lax.fori_loop(0, ngroups, row_body, 0, unroll=2 if ngroups % 2 == 0 else 1)


def _conv_ln_swish(um, ux, w_dw, b_dw, ln_g, ln_b, *, seq, meta_mode):
    cur = ux if meta_mode else um
    m, d = cur.shape
    width = w_dw.shape[0]
    nslab = d // LANES
    tt = N_META if meta_mode else min(256, seq)
    tiles_per_seq = seq // tt
    hb = tt // HALO
    nblk = um.shape[0] // HALO
    if meta_mode:
        prev_map = lambda i: (0, 0)
        next_map = lambda i: (i * (seq // HALO), 0)
        meta_map = lambda i: (i, 0)
    else:
        prev_map = lambda i: (jnp.maximum(i * hb - 1, 0), 0)
        next_map = lambda i: (jnp.minimum((i + 1) * hb, nblk - 1), 0)
        meta_map = lambda i: (i // tiles_per_seq, 0)
    slabs = lambda v: v.astype(F32).reshape(1, nslab, LANES).transpose(1, 0, 2)
    w3 = jnp.pad(w_dw.astype(F32), ((0, 2 * HALO - width), (0, 0)))
    w3 = w3.reshape(2 * HALO, nslab, LANES).transpose(1, 0, 2)
    full3 = lambda a: pl.BlockSpec(a.shape, lambda i: (0, 0, 0))
    consts = [w3, slabs(b_dw), slabs(ln_g), slabs(ln_b)]
    return pl.pallas_call(
        functools.partial(_conv_kernel, width=width, tiles_per_seq=tiles_per_seq, meta_mode=meta_mode),
        grid=(m // tt,),
        in_specs=[pl.BlockSpec((tt, d), lambda i: (i, 0)),
                  pl.BlockSpec((HALO, d), prev_map),
                  pl.BlockSpec((HALO, d), next_map),
                  pl.BlockSpec((N_META, d), meta_map)] + [full3(a) for a in consts],
        out_specs=pl.BlockSpec((tt, d), lambda i: (i, 0)),
        out_shape=jax.ShapeDtypeStruct((m, d), BF16),
        scratch_shapes=[pltpu.VMEM((tt + 2 * HALO, LANES), F32), pltpu.VMEM((nslab, tt, LANES), F32)],
        compiler_params=_cparams(("parallel",)),
        name="conv_ln_swish",
    )(cur, um, um, ux, *consts)


def _dot_nt(a, b):
    return lax.dot_general(a, b, (((1,), (1,)), ((), ())), preferred_element_type=F32)


def _dot_tn(a, b):
    return lax.dot_general(a, b, (((0,), (0,)), ((), ())), preferred_element_type=F32)


def _gla_chunk(q, k, v, g, st_ref, *, rev):
    c, dk = q.shape
    row = lax.broadcasted_iota(jnp.int32, (c, c), 0)
    col = lax.broadcasted_iota(jnp.int32, (c, c), 1)
    tri = ((col >= row) if rev else (col <= row)).astype(BF16)
    g_hi = g.astype(BF16)
    r1 = g - g_hi.astype(F32)
    g_mid = r1.astype(BF16)
    g_lo = (r1 - g_mid.astype(F32)).astype(BF16)
    cum = (jnp.dot(tri, g_hi, preferred_element_type=F32)
           + jnp.dot(tri, g_mid, preferred_element_type=F32)
           + jnp.dot(tri, g_lo, preferred_element_type=F32))
    cum = cum * LOG2_E
    tot = cum[0:1] if rev else cum[c - 1:c]

    st = st_ref[...]
    o = _dot_nt((q * jnp.exp2(cum)).astype(BF16), st.astype(BF16))

    nsb = c // GLA_SUB
    rowl = lax.broadcasted_iota(jnp.int32, (GLA_SUB, c), 0)
    coll = lax.broadcasted_iota(jnp.int32, (GLA_SUB, c), 1)
    blocks = []
    for blk in range(nsb):
        r0 = blk * GLA_SUB
        q_b = q[r0:r0 + GLA_SUB]
        cum_b = cum[r0:r0 + GLA_SUB]
        has_off = (blk < nsb - 1) if rev else (blk > 0)
        if has_off:
            if rev:
                ref = cum[r0 + GLA_SUB:r0 + GLA_SUB + 1]
                k_o, cum_o = k[r0 + GLA_SUB:], cum[r0 + GLA_SUB:]
            else:
                ref = cum[r0 - 1:r0]
                k_o, cum_o = k[:r0], cum[:r0]
            qn = (q_b * jnp.exp2(cum_b - ref)).astype(BF16)
            kn = (k_o * jnp.exp2(ref - cum_o)).astype(BF16)
            pad = jnp.zeros((c - kn.shape[0], dk), BF16)
            kn = jnp.concatenate([pad, kn] if rev else [kn, pad], axis=0)
            a = _dot_nt(qn, kn)
        else:
            a = jnp.zeros((GLA_SUB, c), F32)
        for jj in range(GLA_SUB):
            j = r0 + jj
            dec = jnp.exp2(cum_b - cum[j:j + 1])
            s = jnp.sum(q_b * dec * k[j:j + 1], axis=-1, keepdims=True)
            seen = (rowl <= jj) if rev else (rowl >= jj)
            a = jnp.where((coll == j) & seen, s, a)
        blocks.append(a)
    scores = jnp.concatenate(blocks, axis=0).astype(BF16)
    o = o + jnp.dot(scores, v, preferred_element_type=F32)

    ke = (k * jnp.exp2(tot - cum)).astype(BF16)
    st_ref[...] = st * jnp.exp2(tot) + _dot_tn(v, ke)
    return o


def _gla_kernel(*refs, rev, nc, final, scale, dk, dv):
    (qm, km, vm, lrm, qx, kx, vx, lrx, wb_ref, bg_ref) = refs[:10]
    pos = 10
    if final:
        obm, obx, ogm, ogx, gn_ref = refs[pos:pos + 5]
        pos += 5
    om, ox, st_ref = refs[pos], refs[pos + 1], refs[pos + 2]

    s = pl.program_id(2)
    chunk = (nc - 1 - s) if rev else s
    is_meta = chunk == 0
    c = qm.shape[0]

    @pl.when(s == 0)
    def _():
        st_ref[...] = jnp.zeros_like(st_ref)

    def pick(x_ref, m_ref, cols=slice(None)):
        return jnp.where(is_meta, x_ref[:, cols], m_ref[:, cols])

    lr = pick(lrx, lrm)
    rows = lax.broadcasted_iota(jnp.int32, (c, dk), 0)
    pad_row = is_meta & (rows < c - N_META)
    outs = []
    for hh in range(st_ref.shape[0]):
        kcols = slice(hh * dk, (hh + 1) * dk)
        vcols = slice(hh * dv, (hh + 1) * dv)
        q = pick(qx, qm, kcols).astype(F32) * scale
        k = pick(kx, km, kcols).astype(F32)
        v = pick(vx, vm, vcols)
        pre = jnp.dot(lr, wb_ref[:, kcols], preferred_element_type=F32) + bg_ref[:, kcols]
        g = (jnp.minimum(pre, 0.0) - jnp.log(1.0 + jnp.exp(-jnp.abs(pre)))) / GLA_GATE_TAU
        g = jnp.where(pad_row, 0.0, g)

        o = _gla_chunk(q, k, v, g, st_ref.at[hh], rev=rev)

        if final:
            o = o + pick(obx, obm, vcols)
            ms = jnp.mean(o * o, axis=-1, keepdims=True)
            og = pick(ogx, ogm, vcols).astype(F32)
            o = o * lax.rsqrt(ms + EPS) * gn_ref[:, vcols] * (og * jax.nn.sigmoid(og))
        outs.append(o)
    o_all = jnp.concatenate(outs, axis=-1)

    @pl.when(is_meta)
    def _():
        ox[...] = o_all.astype(ox.dtype)

    @pl.when(jnp.logical_not(is_meta))
    def _():
        om[...] = o_all.astype(om.dtype)


def _gla_pass(proj_m, proj_x, lr_m, lr_x, wb_pad, b_gate, *, batch, rev, final=None):
    rows, width = proj_m.shape
    key_dim = wb_pad.shape[1]
    h = GLA_HEADS
    hb = GLA_HEADS_PER_STEP
    dk = key_dim // h
    dv = (width - 2 * key_dim) // 2 // h
    c = GLA_CHUNK
    nmain = rows // batch // c
    nc = nmain + 1
    ng = h // hb
    wk, wv = hb * dk, hb * dv
    v_col = 2 * key_dim // wv
    g_col = (2 * key_dim + h * dv) // wv
    assert v_col * wv == 2 * key_dim

    def chunk_of(s):
        return (nc - 1 - s) if rev else s

    def main_row(b, s):
        return b * nmain + jnp.maximum(chunk_of(s) - 1, 0)

    def main_spec(w, col0):
        return pl.BlockSpec((c, w), lambda b, hg, s: (main_row(b, s), col0 + hg))

    def meta_spec(w, col0):
        return pl.BlockSpec((c, w), lambda b, hg, s: (b, col0 + hg))

    in_specs = [main_spec(wk, 0), main_spec(wk, ng), main_spec(wv, v_col),
                pl.BlockSpec((c, LANES), lambda b, hg, s: (main_row(b, s), 0)),
                meta_spec(wk, 0), meta_spec(wk, ng), meta_spec(wv, v_col),
                pl.BlockSpec((c, LANES), lambda b, hg, s: (b, 0)),
                pl.BlockSpec((LANES, wk), lambda b, hg, s: (0, hg)),
                pl.BlockSpec((1, wk), lambda b, hg, s: (0, hg))]
    args = [proj_m, proj_m, proj_m, lr_m, proj_x, proj_x, proj_x, lr_x,
            wb_pad, b_gate.reshape(1, key_dim).astype(F32)]
    out_dtype = F32
    if final is not None:
        ob_m, ob_x, g_norm = final
        in_specs += [main_spec(wv, 0), meta_spec(wv, 0), main_spec(wv, g_col), meta_spec(wv, g_col),
                     pl.BlockSpec((1, wv), lambda b, hg, s: (0, hg))]
        args += [ob_m, ob_x, proj_m, proj_x, g_norm.reshape(1, h * dv).astype(F32)]
        out_dtype = BF16
    return pl.pallas_call(
        functools.partial(_gla_kernel, rev=rev, nc=nc, final=final is not None, scale=dk ** -0.5,
                          dk=dk, dv=dv),
        grid=(batch, ng, nc),
        in_specs=in_specs,
        out_specs=[main_spec(wv, 0), meta_spec(wv, 0)],
        out_shape=[jax.ShapeDtypeStruct((rows, h * dv), out_dtype),
                   jax.ShapeDtypeStruct((batch * c, h * dv), out_dtype)],
        scratch_shapes=[pltpu.VMEM((hb, dv, dk), F32)],
        compiler_params=_cparams(("parallel", "parallel", "arbitrary")),
        name="gla_rev" if rev else "gla_fwd",
    )(*args)


def _pad_meta_rows(x, batch):
    w = x.shape[1]
    x = x.reshape(batch, N_META, w)
    x = jnp.pad(x, ((0, 0), (GLA_CHUNK - N_META, 0), (0, 0)))
    return x.reshape(batch * GLA_CHUNK, w)


def _trunk(x, p):
    batch, seq, d = x.shape
    hm = x.reshape(batch * seq, d)
    hx = jnp.broadcast_to(p["meta_tokens"][None], (batch, N_META, d)).reshape(batch * N_META, d)

    zm = _rmsnorm(hm, p["norm_mix"][0], BF16)
    zx = _rmsnorm(hx, p["norm_mix"][0], BF16)
    um = _matmul_glu(zm, p["conv_w_pw1"], p["conv_b_pw1"])
    ux = _matmul_glu(zx, p["conv_w_pw1"], p["conv_b_pw1"])
    conv_args = (p["conv_w_dw"], p["conv_b_dw"], p["conv_ln_g"], p["conv_ln_b"])
    cm = _conv_ln_swish(um, ux, *conv_args, seq=seq, meta_mode=False)
    cx = _conv_ln_swish(um, ux, *conv_args, seq=seq, meta_mode=True)
    hm = _matmul(cm, p["conv_w_pw2"], kind="residual", bias=p["conv_b_pw2"], residual=hm, out_dtype=F32)
    hx = _matmul(cx, p["conv_w_pw2"], kind="residual", bias=p["conv_b_pw2"], residual=hx, out_dtype=F32)

    def mlp(h, i):
        z = _rmsnorm(h, p["norm_mlp"][i], BF16)
        u = _matmul(z, p["mlp_w1"][i], kind="relu2")
        return _matmul(u, p["mlp_w2"][i], kind="residual", residual=h, out_dtype=F32)

    hm = mlp(hm, 0)
    hx = mlp(hx, 0)

    zm = _rmsnorm(hm, p["norm_mix"][1], BF16)
    zx = _rmsnorm(hx, p["norm_mix"][1], BF16)
    proj_m = _matmul(zm, p["gla_w_in"])
    proj_x = _pad_meta_rows(_matmul(zx, p["gla_w_in"]), batch)
    lr_m = _matmul(zm, p["gla_wa"], out_dtype=F32)
    lr_x = _pad_meta_rows(_matmul(zx, p["gla_wa"], out_dtype=F32), batch)

    ob_m, ob_x = _gla_pass(proj_m, proj_x, lr_m, lr_x, p["gla_wb"][1], p["gla_b_gate"][1],
                           batch=batch, rev=True)
    gm, _ = _gla_pass(proj_m, proj_x, lr_m, lr_x, p["gla_wb"][0], p["gla_b_gate"][0],
                      batch=batch, rev=False, final=(ob_m, ob_x, p["gla_norm_g"]))
    hm = _matmul(gm, p["gla_w_out"], kind="residual", residual=hm, out_dtype=F32)
    hm = mlp(hm, 1)

    y = _rmsnorm(hm, p["norm_final"], F32)
    return y.reshape(batch, seq, d)


def kernel(x_prompt, x_sample, meta_tokens, norm_mix, norm_mlp, norm_final, conv_w_pw1, conv_b_pw1, conv_w_dw, conv_b_dw, conv_ln_g, conv_ln_b, conv_w_pw2, conv_b_pw2, gla_w_in, gla_w_gate_a, gla_w_gate_b, gla_b_gate, gla_norm_g, gla_w_out, mlp_w1, mlp_w2):
    assert norm_mix.shape[0] == 2 and conv_w_pw1.shape[0] == 1 and gla_w_in.shape[0] == 1
    d = x_prompt.shape[-1]
    rank = gla_w_gate_a.shape[-1]
    wa = jnp.concatenate([gla_w_gate_a[0, 0], gla_w_gate_a[0, 1],
                          jnp.zeros((d, LANES - 2 * rank), F32)], axis=1).astype(BF16)
    wb = jnp.stack([
        jnp.pad(gla_w_gate_b[0, z], ((z * rank, LANES - (z + 1) * rank), (0, 0))) for z in range(2)])
    p = dict(
        meta_tokens=meta_tokens, norm_mix=norm_mix, norm_mlp=norm_mlp, norm_final=norm_final,
        conv_w_pw1=conv_w_pw1[0].astype(BF16), conv_b_pw1=conv_b_pw1[0],
        conv_w_dw=conv_w_dw[0], conv_b_dw=conv_b_dw[0], conv_ln_g=conv_ln_g[0], conv_ln_b=conv_ln_b[0],
        conv_w_pw2=conv_w_pw2[0].astype(BF16), conv_b_pw2=conv_b_pw2[0],
        gla_w_in=gla_w_in[0].astype(BF16), gla_wa=wa, gla_wb=wb, gla_b_gate=gla_b_gate[0],
        gla_norm_g=gla_norm_g[0], gla_w_out=gla_w_out[0].astype(BF16),
        mlp_w1=mlp_w1.astype(BF16), mlp_w2=mlp_w2.astype(BF16),
    )
    return (_trunk(x_prompt, p), _trunk(x_sample, p))
```

```python
import functools

import jax
import jax.numpy as jnp
from jax import lax
from jax.experimental import pallas as pl
from jax.experimental.pallas import tpu as pltpu

N_META = 16
GLA_HEADS = 4
GLA_HEADS_PER_STEP = 1
GLA_GATE_TAU = 16.0
GLA_STEP = 256
GLA_CHUNK = 64
GLA_SUB = 16
EPS = 1e-6
LOG2_E = 1.4426950408889634
HALO = 16
SUBLANES = 8
LANES = 128
VMEM_LIMIT = 52 * 1024 * 1024
MAX_FULL_K = 4096

F32 = jnp.float32
BF16 = jnp.bfloat16


def _cparams(sem):
    return pltpu.CompilerParams(dimension_semantics=sem, vmem_limit_bytes=VMEM_LIMIT)


def _rmsnorm_kernel(x_ref, g_ref, o_ref):
    x = x_ref[...].astype(F32)
    ms = jnp.mean(x * x, axis=-1, keepdims=True)
    o_ref[...] = (x * lax.rsqrt(ms + EPS) * g_ref[...]).astype(o_ref.dtype)


def _rmsnorm(x, g, out_dtype):
    m, d = x.shape
    tr = min(256, m)
    return pl.pallas_call(
        _rmsnorm_kernel,
        grid=(m // tr,),
        in_specs=[pl.BlockSpec((tr, d), lambda i: (i, 0)),
                  pl.BlockSpec((1, d), lambda i: (0, 0))],
        out_specs=pl.BlockSpec((tr, d), lambda i: (i, 0)),
        out_shape=jax.ShapeDtypeStruct((m, d), out_dtype),
        compiler_params=_cparams(("parallel",)),
        name="rmsnorm",
    )(x, g.reshape(1, d).astype(F32))


def _mm_tiles(m, n, k, wide):
    if k > MAX_FULL_K:
        return min(1024, m), min(1024, n), MAX_FULL_K // 2
    return min(1024, m), min(1024 if wide else 512, n), k


def _mm_kernel(*refs, nk, kind, has_bias, has_ss, emit_norm_inputs):
    x_ref, w_ref = refs[0], refs[1]
    pos = 2
    b_ref = r_ref = ss_ref = None
    if has_bias:
        b_ref = refs[pos]
        pos += 1
    if kind == "residual":
        r_ref = refs[pos]
        pos += 1
    if has_ss:
        ss_ref = refs[pos]
        pos += 1
    o_ref = refs[pos]

    def prod():
        return jnp.dot(x_ref[...], w_ref[...], preferred_element_type=F32)

    if kind == "residual":
        def base():
            return r_ref[...] + b_ref[...] if has_bias else r_ref[...]

        def emit(y):
            hb_ref, sso_ref = refs[pos + 1], refs[pos + 2]
            hb_ref[...] = y.astype(hb_ref.dtype)
            piece = jnp.broadcast_to(jnp.sum(y * y, axis=-1, keepdims=True), sso_ref.shape)

            @pl.when(pl.program_id(1) == 0)
            def _():
                sso_ref[...] = piece

            @pl.when(pl.program_id(1) > 0)
            def _():
                sso_ref[...] += piece

        if nk == 1:
            y = base() + prod()
            o_ref[...] = y
            if emit_norm_inputs:
                emit(y)
        else:
            @pl.when(pl.program_id(2) == 0)
            def _():
                o_ref[...] = base()

            o_ref[...] += prod()
            if emit_norm_inputs:
                @pl.when(pl.program_id(2) == nk - 1)
                def _():
                    emit(o_ref[...])
        return

    assert nk == 1
    y = prod()
    if has_ss:
        ms = ss_ref[:, 0:1] / x_ref.shape[1]
        y = y * lax.rsqrt(ms + EPS)
    if has_bias:
        y = y + b_ref[...]
    if kind == "relu2":
        y = jnp.square(jnp.maximum(y, 0.0))
    o_ref[...] = y.astype(o_ref.dtype)


def _matmul(x, w, *, kind="plain", bias=None, residual=None, row_ss=None, emit_norm_inputs=False,
            out_dtype=BF16):
    m, kdim = x.shape
    n = w.shape[1]
    tm, tn, tk = _mm_tiles(m, n, kdim, wide=kind != "residual")
    nk = kdim // tk
    in_specs = [pl.BlockSpec((tm, tk), lambda i, j, k: (i, k)),
                pl.BlockSpec((tk, tn), lambda i, j, k: (k, j))]
    args = [x, w]
    if bias is not None:
        in_specs.append(pl.BlockSpec((1, tn), lambda i, j, k: (0, j)))
        args.append(bias.reshape(1, n).astype(F32))
    if kind == "residual":
        in_specs.append(pl.BlockSpec((tm, tn), lambda i, j, k: (i, j)))
        args.append(residual)
    if row_ss is not None:
        in_specs.append(pl.BlockSpec((tm, row_ss.shape[1]), lambda i, j, k: (i, 0)))
        args.append(row_ss)
    tile = pl.BlockSpec((tm, tn), lambda i, j, k: (i, j))
    out_specs, out_shape = tile, jax.ShapeDtypeStruct((m, n), out_dtype)
    if emit_norm_inputs:
        out_specs = [tile, tile, pl.BlockSpec((tm, LANES), lambda i, j, k: (i, 0))]
        out_shape = [out_shape, jax.ShapeDtypeStruct((m, n), BF16), jax.ShapeDtypeStruct((m, LANES), F32)]
    return pl.pallas_call(
        functools.partial(_mm_kernel, nk=nk, kind=kind, has_bias=bias is not None,
                          has_ss=row_ss is not None, emit_norm_inputs=emit_norm_inputs),
        grid=(m // tm, n // tn, nk),
        in_specs=in_specs,
        out_specs=out_specs,
        out_shape=out_shape,
        compiler_params=_cparams(("parallel", "arbitrary" if emit_norm_inputs else "parallel", "arbitrary")),
        name="matmul_" + kind,
    )(*args)


def _glu_kernel(x_ref, wa_ref, wg_ref, ba_ref, bg_ref, o_ref):
    x = x_ref[...]
    a = jnp.dot(x, wa_ref[...], preferred_element_type=F32) + ba_ref[...]
    g = jnp.dot(x, wg_ref[...], preferred_element_type=F32) + bg_ref[...]
    o_ref[...] = (a * jax.nn.sigmoid(g)).astype(o_ref.dtype)


def _matmul_glu(x, w, bias):
    m, kdim = x.shape
    n = w.shape[1] // 2
    tm, tn, tk = _mm_tiles(m, n, kdim, wide=False)
    assert tk == kdim
    nj = n // tn
    b = bias.reshape(1, 2 * n).astype(F32)
    return pl.pallas_call(
        _glu_kernel,
        grid=(m // tm, nj),
        in_specs=[pl.BlockSpec((tm, tk), lambda i, j: (i, 0)),
                  pl.BlockSpec((tk, tn), lambda i, j: (0, j)),
                  pl.BlockSpec((tk, tn), lambda i, j: (0, j + nj)),
                  pl.BlockSpec((1, tn), lambda i, j: (0, j)),
                  pl.BlockSpec((1, tn), lambda i, j: (0, j + nj))],
        out_specs=pl.BlockSpec((tm, tn), lambda i, j: (i, j)),
        out_shape=jax.ShapeDtypeStruct((m, n), BF16),
        compiler_params=_cparams(("parallel", "parallel")),
        name="matmul_glu",
    )(x, w, w, b, b)


def _row_strides(tt):
    n = tt // SUBLANES
    if n % SUBLANES:
        return [n]
    a = n // 2 + 1
    return [a, n - a]


def _conv_kernel(cur_ref, prev_ref, next_ref, meta_ref, w_ref, b_ref, g_ref, beta_ref, o_ref,
                 ext_ref, conv_ref, *, width, tiles_per_seq, meta_mode):
    tt, d = cur_ref.shape
    nslab = d // LANES
    i = pl.program_id(0)
    shift = HALO - width // 2
    strides = _row_strides(tt)
    if not meta_mode:
        first = i % tiles_per_seq == 0
        last = i % tiles_per_seq == tiles_per_seq - 1

    def col_body(cb, carry):
        lanes = pl.ds(pl.multiple_of(cb * LANES, LANES), LANES)
        if meta_mode:
            before = jnp.zeros((HALO, LANES), F32)
            after = next_ref[:, lanes].astype(F32)
        else:
            before = jnp.where(first, meta_ref[:, lanes], prev_ref[:, lanes]).astype(F32)
            after = jnp.where(last, 0.0, next_ref[:, lanes].astype(F32))
        ext_ref[0:HALO, :] = before
        ext_ref[HALO:HALO + tt, :] = cur_ref[:, lanes].astype(F32)
        ext_ref[HALO + tt:HALO + tt + HALO, :] = after
        taps_w = [jnp.broadcast_to(w_ref[cb, j:j + 1, :], (SUBLANES, LANES)) for j in range(width)]
        bias = b_ref[cb]
        base = 0
        for s in strides:
            for r in range(s):
                acc = None
                for j in range(width):
                    tap = ext_ref[pl.ds(base + r + j + shift, SUBLANES, stride=s), :]
                    acc = tap * taps_w[j] if acc is None else acc + tap * taps_w[j]
                conv_ref[cb, pl.ds(base + r, SUBLANES, stride=s), :] = acc + bias
            base += SUBLANES * s
        return carry

    lax.fori_loop(0, nslab, col_body, 0)

    rows_per_group = 2 * SUBLANES

    def row_body(gi, carry):
        rows = pl.ds(pl.multiple_of(gi * rows_per_group, rows_per_group), rows_per_group)
        u = conv_ref[:, rows, :]
        mu = jnp.sum(jnp.sum(u, axis=0), axis=-1, keepdims=True) / d
        uc = u - mu
        var = jnp.sum(jnp.sum(uc * uc, axis=0), axis=-1, keepdims=True) / d
        y = uc * lax.rsqrt(var + EPS) * g_ref[...] + beta_ref[...]
        y = y * jax.nn.sigmoid(y)
        o_ref[rows, :] = jnp.concatenate([y[c] for c in range(nslab)], axis=-1).astype(o_ref.dtype)
        return carry

    ngroups = tt // rows_per_group
    lax.fori_loop(0, ngroups, row_body, 0, unroll=2 if ngroups % 2 == 0 else 1)


def _conv_ln_swish(um, ux, w_dw, b_dw, ln_g, ln_b, *, seq, meta_mode):
    cur = ux if meta_mode else um
    m, d = cur.shape
    width = w_dw.shape[0]
    nslab = d // LANES
    tt = N_META if meta_mode else min(256, seq)
    tiles_per_seq = seq // tt
    hb = tt // HALO
    nblk = um.shape[0] // HALO
    if meta_mode:
        prev_map = lambda i: (0, 0)
        next_map = lambda i: (i * (seq // HALO), 0)
        meta_map = lambda i: (i, 0)
    else:
        prev_map = lambda i: (jnp.maximum(i * hb - 1, 0), 0)
        next_map = lambda i: (jnp.minimum((i + 1) * hb, nblk - 1), 0)
        meta_map = lambda i: (i // tiles_per_seq, 0)
    slabs = lambda v: v.astype(F32).reshape(1, nslab, LANES).transpose(1, 0, 2)
    w3 = jnp.pad(w_dw.astype(F32), ((0, 2 * HALO - width), (0, 0)))
    w3 = w3.reshape(2 * HALO, nslab, LANES).transpose(1, 0, 2)
    full3 = lambda a: pl.BlockSpec(a.shape, lambda i: (0, 0, 0))
    consts = [w3, slabs(b_dw), slabs(ln_g), slabs(ln_b)]
    return pl.pallas_call(
        functools.partial(_conv_kernel, width=width, tiles_per_seq=tiles_per_seq, meta_mode=meta_mode),
        grid=(m // tt,),
        in_specs=[pl.BlockSpec((tt, d), lambda i: (i, 0)),
                  pl.BlockSpec((HALO, d), prev_map),
                  pl.BlockSpec((HALO, d), next_map),
                  pl.BlockSpec((N_META, d), meta_map)] + [full3(a) for a in consts],
        out_specs=pl.BlockSpec((tt, d), lambda i: (i, 0)),
        out_shape=jax.ShapeDtypeStruct((m, d), BF16),
        scratch_shapes=[pltpu.VMEM((tt + 2 * HALO, LANES), F32), pltpu.VMEM((nslab, tt, LANES), F32)],
        compiler_params=_cparams(("parallel",)),
        name="conv_ln_swish",
    )(cur, um, um, ux, *consts)


def _dot_nt(a, b):
    return lax.dot_general(a, b, (((1,), (1,)), ((), ())), preferred_element_type=F32)


def _dot_tn(a, b):
    return lax.dot_general(a, b, (((0,), (0,)), ((), ())), preferred_element_type=F32)


def _padded_rows(x, r0, total):
    parts = []
    if r0:
        parts.append(jnp.zeros((r0, x.shape[1]), x.dtype))
    parts.append(x)
    tail = total - r0 - x.shape[0]
    if tail:
        parts.append(jnp.zeros((tail, x.shape[1]), x.dtype))
    return parts[0] if len(parts) == 1 else jnp.concatenate(parts, axis=0)


def _gla_step(q, k, v, g, st_ref, *, rev):
    r, dk = q.shape
    c, sb = GLA_CHUNK, GLA_SUB
    nb = r // c
    row = lax.broadcasted_iota(jnp.int32, (r, r), 0)
    col = lax.broadcasted_iota(jnp.int32, (r, r), 1)
    in_chunk = (row // c) == (col // c)
    tri = (in_chunk & ((col >= row) if rev else (col <= row))).astype(BF16)
    g_hi = g.astype(BF16)
    r1 = g - g_hi.astype(F32)
    g_mid = r1.astype(BF16)
    g_lo = (r1 - g_mid.astype(F32)).astype(BF16)
    lc = (jnp.dot(tri, g_hi, preferred_element_type=F32)
          + jnp.dot(tri, g_mid, preferred_element_type=F32)
          + jnp.dot(tri, g_lo, preferred_element_type=F32)) * LOG2_E

    chunks = [slice(b * c, (b + 1) * c) for b in range(nb)]
    tot = [lc[b * c:b * c + 1] if rev else lc[(b + 1) * c - 1:(b + 1) * c] for b in range(nb)]
    before = [None] * nb
    run = jnp.zeros((1, dk), F32)
    for b in (reversed(range(nb)) if rev else range(nb)):
        before[b] = run
        run = run + tot[b]
    total = run

    def earlier(j, i):
        return j > i if rev else j < i

    qn = q * jnp.exp2(lc)
    kn = [k[chunks[b]] * jnp.exp2(tot[b] - lc[chunks[b]]) for b in range(nb)]

    st = st_ref[...]
    qe = jnp.concatenate([qn[chunks[b]] * jnp.exp2(before[b]) for b in range(nb)], axis=0)
    o = _dot_nt(qe.astype(BF16), st.astype(BF16))

    rowl = lax.broadcasted_iota(jnp.int32, (SUBLANES, r), 0)
    coll = lax.broadcasted_iota(jnp.int32, (SUBLANES, r), 1)
    halves = sb // SUBLANES
    score_rows = []
    for b in range(nb):
        qn_b = qn[chunks[b]].astype(BF16)
        parts = [(kn[j] * jnp.exp2(before[b] - before[j] - tot[j])).astype(BF16) if earlier(j, b)
                 else jnp.zeros((c, dk), BF16) for j in range(nb)]
        cross = _dot_nt(qn_b, jnp.concatenate(parts, axis=0)) if nb > 1 else None
        q_c, k_c, lc_c = q[chunks[b]], k[chunks[b]], lc[chunks[b]]
        nsb = c // sb
        for blk in range(nsb):
            r0 = blk * sb
            q_b, lc_b = q_c[r0:r0 + sb], lc_c[r0:r0 + sb]
            a = None if cross is None else cross[r0:r0 + sb]
            if (blk < nsb - 1) if rev else (blk > 0):
                if rev:
                    ref = lc_c[r0 + sb:r0 + sb + 1]
                    k_o, lc_o, at = k_c[r0 + sb:], lc_c[r0 + sb:], b * c + r0 + sb
                else:
                    ref = lc_c[r0 - 1:r0]
                    k_o, lc_o, at = k_c[:r0], lc_c[:r0], b * c
                qs = (q_b * jnp.exp2(lc_b - ref)).astype(BF16)
                ks = (k_o * jnp.exp2(ref - lc_o)).astype(BF16)
                near = _dot_nt(qs, _padded_rows(ks, at, r))
                a = near if a is None else a + near
            if a is None:
                a = jnp.zeros((sb, r), F32)
            a_h = [a[h * SUBLANES:(h + 1) * SUBLANES] for h in range(halves)]
            for jj in range(sb):
                j = r0 + jj
                for h in range(halves):
                    lo = h * SUBLANES
                    if (lo > jj) if rev else (lo + SUBLANES - 1 < jj):
                        continue
                    dec = jnp.exp2(lc_b[lo:lo + SUBLANES] - lc_c[j:j + 1])
                    s = jnp.sum(q_b[lo:lo + SUBLANES] * dec * k_c[j:j + 1], axis=-1, keepdims=True)
                    seen = (rowl + lo <= jj) if rev else (rowl + lo >= jj)
                    a_h[h] = jnp.where((coll == b * c + j) & seen, s, a_h[h])
            score_rows += a_h
    scores = jnp.concatenate(score_rows, axis=0).astype(BF16)
    o = o + jnp.dot(scores, v, preferred_element_type=F32)

    ke = jnp.concatenate([kn[b] * jnp.exp2(total - before[b] - tot[b]) for b in range(nb)], axis=0)
    st_ref[...] = st * jnp.exp2(total) + _dot_tn(v, ke.astype(BF16))
    return o


def _gla_kernel(*refs, rev, nc, final, scale, dk, dv):
    (qm, km, vm, lrm, qx, kx, vx, lrx, wb_ref, bg_ref) = refs[:10]
    pos = 10
    if final:
        obm, obx, ogm, ogx, gn_ref = refs[pos:pos + 5]
        pos += 5
    om, ox, st_ref = refs[pos], refs[pos + 1], refs[pos + 2]

    s = pl.program_id(2)
    chunk = (nc - 1 - s) if rev else s
    is_meta = chunk == 0
    c = qm.shape[0]

    @pl.when(s == 0)
    def _():
        st_ref[...] = jnp.zeros_like(st_ref)

    def pick(x_ref, m_ref, cols=slice(None)):
        return jnp.where(is_meta, x_ref[:, cols], m_ref[:, cols])

    lr = pick(lrx, lrm)
    rows = lax.broadcasted_iota(jnp.int32, (c, dk), 0)
    pad_row = is_meta & (rows < c - N_META)
    outs = []
    for hh in range(st_ref.shape[0]):
        kcols = slice(hh * dk, (hh + 1) * dk)
        vcols = slice(hh * dv, (hh + 1) * dv)
        q = pick(qx, qm, kcols).astype(F32) * scale
        k = pick(kx, km, kcols).astype(F32)
        v = pick(vx, vm, vcols)
        pre = jnp.dot(lr, wb_ref[:, kcols], preferred_element_type=F32) + bg_ref[:, kcols]
        g = (jnp.minimum(pre, 0.0) - jnp.log(1.0 + jnp.exp(-jnp.abs(pre)))) / GLA_GATE_TAU
        g = jnp.where(pad_row, 0.0, g)

        o = _gla_step(q, k, v, g, st_ref.at[hh], rev=rev)

        if final:
            o = o + pick(obx, obm, vcols)
            ms = jnp.mean(o * o, axis=-1, keepdims=True)
            og = pick(ogx, ogm, vcols).astype(F32)
            o = o * lax.rsqrt(ms + EPS) * gn_ref[:, vcols] * (og * jax.nn.sigmoid(og))
        outs.append(o)
    o_all = jnp.concatenate(outs, axis=-1)

    @pl.when(is_meta)
    def _():
        ox[...] = o_all.astype(ox.dtype)

    @pl.when(jnp.logical_not(is_meta))
    def _():
        om[...] = o_all.astype(om.dtype)


def _gla_pass(proj_m, proj_x, lr_m, lr_x, wb_pad, b_gate, *, batch, rev, final=None):
    rows, width = proj_m.shape
    key_dim = wb_pad.shape[1]
    h = GLA_HEADS
    hb = GLA_HEADS_PER_STEP
    dk = key_dim // h
    dv = (width - 2 * key_dim) // 2 // h
    c = GLA_STEP
    nmain = rows // batch // c
    nc = nmain + 1
    ng = h // hb
    wk, wv = hb * dk, hb * dv
    v_col = 2 * key_dim // wv
    g_col = (2 * key_dim + h * dv) // wv
    assert v_col * wv == 2 * key_dim

    def chunk_of(s):
        return (nc - 1 - s) if rev else s

    def main_row(b, s):
        return b * nmain + jnp.maximum(chunk_of(s) - 1, 0)

    def main_spec(w, col0):
        return pl.BlockSpec((c, w), lambda b, hg, s: (main_row(b, s), col0 + hg))

    def meta_spec(w, col0):
        return pl.BlockSpec((c, w), lambda b, hg, s: (b, col0 + hg))

    in_specs = [main_spec(wk, 0), main_spec(wk, ng), main_spec(wv, v_col),
                pl.BlockSpec((c, LANES), lambda b, hg, s: (main_row(b, s), 0)),
                meta_spec(wk, 0), meta_spec(wk, ng), meta_spec(wv, v_col),
                pl.BlockSpec((c, LANES), lambda b, hg, s: (b, 0)),
                pl.BlockSpec((LANES, wk), lambda b, hg, s: (0, hg)),
                pl.BlockSpec((1, wk), lambda b, hg, s: (0, hg))]
    args = [proj_m, proj_m, proj_m, lr_m, proj_x, proj_x, proj_x, lr_x,
            wb_pad, b_gate.reshape(1, key_dim).astype(F32)]
    out_dtype = F32
    if final is not None:
        ob_m, ob_x, g_norm = final
        in_specs += [main_spec(wv, 0), meta_spec(wv, 0), main_spec(wv, g_col), meta_spec(wv, g_col),
                     pl.BlockSpec((1, wv), lambda b, hg, s: (0, hg))]
        args += [ob_m, ob_x, proj_m, proj_x, g_norm.reshape(1, h * dv).astype(F32)]
        out_dtype = BF16
    return pl.pallas_call(
        functools.partial(_gla_kernel, rev=rev, nc=nc, final=final is not None, scale=dk ** -0.5,
                          dk=dk, dv=dv),
        grid=(batch, ng, nc),
        in_specs=in_specs,
        out_specs=[main_spec(wv, 0), meta_spec(wv, 0)],
        out_shape=[jax.ShapeDtypeStruct((rows, h * dv), out_dtype),
                   jax.ShapeDtypeStruct((batch * c, h * dv), out_dtype)],
        scratch_shapes=[pltpu.VMEM((hb, dv, dk), F32)],
        compiler_params=_cparams(("parallel", "parallel", "arbitrary")),
        name="gla_rev" if rev else "gla_fwd",
    )(*args)


def _pad_meta_rows(x, batch):
    w = x.shape[1]
    x = x.reshape(batch, N_META, w)
    x = jnp.pad(x, ((0, 0), (GLA_STEP - N_META, 0), (0, 0)))
    return x.reshape(batch * GLA_STEP, w)


def _trunk(x, p):
    batch, seq, d = x.shape
    hm = x.reshape(batch * seq, d)
    hx = jnp.broadcast_to(p["meta_tokens"][None], (batch, N_META, d)).reshape(batch * N_META, d)

    zm = _rmsnorm(hm, p["norm_mix"][0], BF16)
    zx = _rmsnorm(hx, p["norm_mix"][0], BF16)
    um = _matmul_glu(zm, p["conv_w_pw1"], p["conv_b_pw1"])
    ux = _matmul_glu(zx, p["conv_w_pw1"], p["conv_b_pw1"])
    conv_args = (p["conv_w_dw"], p["conv_b_dw"], p["conv_ln_g"], p["conv_ln_b"])
    cm = _conv_ln_swish(um, ux, *conv_args, seq=seq, meta_mode=False)
    cx = _conv_ln_swish(um, ux, *conv_args, seq=seq, meta_mode=True)
    def residual(a, w, s, emit=True, **kw):
        return _matmul(a, w, kind="residual", residual=s[0], out_dtype=F32, emit_norm_inputs=emit, **kw)

    def mlp(s, i, emit):
        u = _matmul(s[1], p["mlp_w1_normed"][i], kind="relu2", row_ss=s[2])
        return residual(u, p["mlp_w2"][i], s, emit)

    sm = residual(cm, p["conv_w_pw2"], (hm,), bias=p["conv_b_pw2"])
    sx = residual(cx, p["conv_w_pw2"], (hx,), bias=p["conv_b_pw2"])
    sm = mlp(sm, 0, True)
    sx = mlp(sx, 0, True)

    proj_m = _matmul(sm[1], p["gla_w_in_normed"], row_ss=sm[2])
    proj_x = _pad_meta_rows(_matmul(sx[1], p["gla_w_in_normed"], row_ss=sx[2]), batch)
    lr_m = _matmul(sm[1], p["gla_wa_normed"], row_ss=sm[2], out_dtype=F32)
    lr_x = _pad_meta_rows(_matmul(sx[1], p["gla_wa_normed"], row_ss=sx[2], out_dtype=F32), batch)

    ob_m, ob_x = _gla_pass(proj_m, proj_x, lr_m, lr_x, p["gla_wb"][1], p["gla_b_gate"][1],
                           batch=batch, rev=True)
    gm, _ = _gla_pass(proj_m, proj_x, lr_m, lr_x, p["gla_wb"][0], p["gla_b_gate"][0],
                      batch=batch, rev=False, final=(ob_m, ob_x, p["gla_norm_g"]))
    sm = residual(gm, p["gla_w_out"], sm)
    hm = mlp(sm, 1, False)

    y = _rmsnorm(hm, p["norm_final"], F32)
    return y.reshape(batch, seq, d)


def kernel(x_prompt, x_sample, meta_tokens, norm_mix, norm_mlp, norm_final, conv_w_pw1, conv_b_pw1, conv_w_dw, conv_b_dw, conv_ln_g, conv_ln_b, conv_w_pw2, conv_b_pw2, gla_w_in, gla_w_gate_a, gla_w_gate_b, gla_b_gate, gla_norm_g, gla_w_out, mlp_w1, mlp_w2):
    assert norm_mix.shape[0] == 2 and conv_w_pw1.shape[0] == 1 and gla_w_in.shape[0] == 1
    d = x_prompt.shape[-1]
    rank = gla_w_gate_a.shape[-1]
    wa = jnp.concatenate([gla_w_gate_a[0, 0], gla_w_gate_a[0, 1],
                          jnp.zeros((d, LANES - 2 * rank), F32)], axis=1)
    wb = jnp.stack([
        jnp.pad(gla_w_gate_b[0, z], ((z * rank, LANES - (z + 1) * rank), (0, 0))) for z in range(2)])

    def normed(gain, w):
        return (gain[:, None] * w).astype(BF16)

    p = dict(
        meta_tokens=meta_tokens, norm_mix=norm_mix, norm_final=norm_final,
        conv_w_pw1=conv_w_pw1[0].astype(BF16), conv_b_pw1=conv_b_pw1[0],
        conv_w_dw=conv_w_dw[0], conv_b_dw=conv_b_dw[0], conv_ln_g=conv_ln_g[0], conv_ln_b=conv_ln_b[0],
        conv_w_pw2=conv_w_pw2[0].astype(BF16), conv_b_pw2=conv_b_pw2[0],
        gla_w_in_normed=normed(norm_mix[1], gla_w_in[0]), gla_wa_normed=normed(norm_mix[1], wa),
        gla_wb=wb, gla_b_gate=gla_b_gate[0],
        gla_norm_g=gla_norm_g[0], gla_w_out=gla_w_out[0].astype(BF16),
        mlp_w1_normed=[normed(norm_mlp[i], mlp_w1[i]) for i in range(2)],
        mlp_w2=[mlp_w2[i].astype(BF16) for i in range(2)],
    )
    return (_trunk(x_prompt, p), _trunk(x_sample, p))
```

```python
import functools

import jax
import jax.numpy as jnp
from jax import lax
from jax.experimental import pallas as pl
from jax.experimental.pallas import tpu as pltpu

N_META = 16
GLA_HEADS = 4
GLA_HEADS_PER_STEP = 1
GLA_GATE_TAU = 16.0
GLA_STEP = 256
GLA_CHUNK = 64
GLA_SUB = 16
EPS = 1e-6
LOG2_E = 1.4426950408889634
HALO = 16
SUBLANES = 8
LANES = 128
VMEM_LIMIT = 52 * 1024 * 1024
MAX_FULL_K = 4096

F32 = jnp.float32
BF16 = jnp.bfloat16


def _cparams(sem):
    return pltpu.CompilerParams(dimension_semantics=sem, vmem_limit_bytes=VMEM_LIMIT)


def _rmsnorm_kernel(x_ref, g_ref, o_ref):
    x = x_ref[...].astype(F32)
    ms = jnp.mean(x * x, axis=-1, keepdims=True)
    o_ref[...] = (x * lax.rsqrt(ms + EPS) * g_ref[...]).astype(o_ref.dtype)


def _rmsnorm(x, g, out_dtype):
    m, d = x.shape
    tr = min(256, m)
    return pl.pallas_call(
        _rmsnorm_kernel,
        grid=(m // tr,),
        in_specs=[pl.BlockSpec((tr, d), lambda i: (i, 0)),
                  pl.BlockSpec((1, d), lambda i: (0, 0))],
        out_specs=pl.BlockSpec((tr, d), lambda i: (i, 0)),
        out_shape=jax.ShapeDtypeStruct((m, d), out_dtype),
        compiler_params=_cparams(("parallel",)),
        name="rmsnorm",
    )(x, g.reshape(1, d).astype(F32))


def _mm_tiles(m, n, k, wide):
    if k > MAX_FULL_K:
        return min(1024, m), min(1024, n), MAX_FULL_K // 2
    return min(1024, m), min(1024 if wide else 512, n), k


def _mm_kernel(*refs, nk, kind, has_bias, has_ss, emit_norm_inputs):
    x_ref, w_ref = refs[0], refs[1]
    pos = 2
    b_ref = r_ref = ss_ref = None
    if has_bias:
        b_ref = refs[pos]
        pos += 1
    if kind == "residual":
        r_ref = refs[pos]
        pos += 1
    if has_ss:
        ss_ref = refs[pos]
        pos += 1
    o_ref = refs[pos]

    def prod():
        return jnp.dot(x_ref[...], w_ref[...], preferred_element_type=F32)

    if kind == "residual":
        def base():
            return r_ref[...] + b_ref[...] if has_bias else r_ref[...]

        def emit(y):
            hb_ref, sso_ref = refs[pos + 1], refs[pos + 2]
            hb_ref[...] = y.astype(hb_ref.dtype)
            piece = jnp.broadcast_to(jnp.sum(y * y, axis=-1, keepdims=True), sso_ref.shape)

            @pl.when(pl.program_id(1) == 0)
            def _():
                sso_ref[...] = piece

            @pl.when(pl.program_id(1) > 0)
            def _():
                sso_ref[...] += piece

        if nk == 1:
            y = base() + prod()
            o_ref[...] = y
            if emit_norm_inputs:
                emit(y)
        else:
            @pl.when(pl.program_id(2) == 0)
            def _():
                o_ref[...] = base()

            o_ref[...] += prod()
            if emit_norm_inputs:
                @pl.when(pl.program_id(2) == nk - 1)
                def _():
                    emit(o_ref[...])
        return

    assert nk == 1
    y = prod()
    if has_ss:
        ms = ss_ref[:, 0:1] / x_ref.shape[1]
        y = y * lax.rsqrt(ms + EPS)
    if has_bias:
        y = y + b_ref[...]
    if kind == "relu2":
        y = jnp.square(jnp.maximum(y, 0.0))
    o_ref[...] = y.astype(o_ref.dtype)


def _matmul(x, w, *, layer=None, kind="plain", bias=None, residual=None, row_ss=None,
            emit_norm_inputs=False, out_dtype=BF16):
    m, kdim = x.shape
    n = w.shape[-1]
    tm, tn, tk = _mm_tiles(m, n, kdim, wide=kind != "residual")
    nk = kdim // tk
    if layer is None:
        w_spec = pl.BlockSpec((tk, tn), lambda i, j, k: (k, j))
    else:
        w_spec = pl.BlockSpec((None, tk, tn), lambda i, j, k: (layer, k, j))
    in_specs = [pl.BlockSpec((tm, tk), lambda i, j, k: (i, k)), w_spec]
    args = [x, w]
    if bias is not None:
        in_specs.append(pl.BlockSpec((1, tn), lambda i, j, k: (0, j)))
        args.append(bias.reshape(1, n).astype(F32))
    if kind == "residual":
        in_specs.append(pl.BlockSpec((tm, tn), lambda i, j, k: (i, j)))
        args.append(residual)
    if row_ss is not None:
        in_specs.append(pl.BlockSpec((tm, row_ss.shape[1]), lambda i, j, k: (i, 0)))
        args.append(row_ss)
    tile = pl.BlockSpec((tm, tn), lambda i, j, k: (i, j))
    out_specs, out_shape = tile, jax.ShapeDtypeStruct((m, n), out_dtype)
    if emit_norm_inputs:
        out_specs = [tile, tile, pl.BlockSpec((tm, LANES), lambda i, j, k: (i, 0))]
        out_shape = [out_shape, jax.ShapeDtypeStruct((m, n), BF16), jax.ShapeDtypeStruct((m, LANES), F32)]
    return pl.pallas_call(
        functools.partial(_mm_kernel, nk=nk, kind=kind, has_bias=bias is not None,
                          has_ss=row_ss is not None, emit_norm_inputs=emit_norm_inputs),
        grid=(m // tm, n // tn, nk),
        in_specs=in_specs,
        out_specs=out_specs,
        out_shape=out_shape,
        compiler_params=_cparams(("parallel", "arbitrary" if emit_norm_inputs else "parallel", "arbitrary")),
        name="matmul_" + kind,
    )(*args)


def _glu_kernel(x_ref, wa_ref, wg_ref, ba_ref, bg_ref, o_ref):
    x = x_ref[...]
    a = jnp.dot(x, wa_ref[...], preferred_element_type=F32) + ba_ref[...]
    g = jnp.dot(x, wg_ref[...], preferred_element_type=F32) + bg_ref[...]
    o_ref[...] = (a * jax.nn.sigmoid(g)).astype(o_ref.dtype)


def _matmul_glu(x, w, bias):
    m, kdim = x.shape
    n = w.shape[1] // 2
    tm, tn, tk = _mm_tiles(m, n, kdim, wide=False)
    assert tk == kdim
    nj = n // tn
    b = bias.reshape(1, 2 * n).astype(F32)
    return pl.pallas_call(
        _glu_kernel,
        grid=(m // tm, nj),
        in_specs=[pl.BlockSpec((tm, tk), lambda i, j: (i, 0)),
                  pl.BlockSpec((tk, tn), lambda i, j: (0, j)),
                  pl.BlockSpec((tk, tn), lambda i, j: (0, j + nj)),
                  pl.BlockSpec((1, tn), lambda i, j: (0, j)),
                  pl.BlockSpec((1, tn), lambda i, j: (0, j + nj))],
        out_specs=pl.BlockSpec((tm, tn), lambda i, j: (i, j)),
        out_shape=jax.ShapeDtypeStruct((m, n), BF16),
        compiler_params=_cparams(("parallel", "parallel")),
        name="matmul_glu",
    )(x, w, w, b, b)


def _row_strides(tt):
    n = tt // SUBLANES
    if n % SUBLANES:
        return [n]
    a = n // 2 + 1
    return [a, n - a]


def _conv_kernel(cur_ref, prev_ref, next_ref, meta_ref, w_ref, b_ref, g_ref, beta_ref, o_ref,
                 ext_ref, conv_ref, *, width, tiles_per_seq, meta_mode):
    tt, d = cur_ref.shape
    nslab = d // LANES
    i = pl.program_id(0)
    shift = HALO - width // 2
    strides = _row_strides(tt)
    if not meta_mode:
        first = i % tiles_per_seq == 0
        last = i % tiles_per_seq == tiles_per_seq - 1

    def col_body(cb, carry):
        lanes = pl.ds(pl.multiple_of(cb * LANES, LANES), LANES)
        if meta_mode:
            before = jnp.zeros((HALO, LANES), F32)
            after = next_ref[:, lanes].astype(F32)
        else:
            before = jnp.where(first, meta_ref[:, lanes], prev_ref[:, lanes]).astype(F32)
            after = jnp.where(last, 0.0, next_ref[:, lanes].astype(F32))
        ext_ref[0:HALO, :] = before
        ext_ref[HALO:HALO + tt, :] = cur_ref[:, lanes].astype(F32)
        ext_ref[HALO + tt:HALO + tt + HALO, :] = after
        taps_w = [jnp.broadcast_to(w_ref[cb, j:j + 1, :], (SUBLANES, LANES)) for j in range(width)]
        bias = b_ref[cb]
        base = 0
        for s in strides:
            for r in range(s):
                acc = None
                for j in range(width):
                    tap = ext_ref[pl.ds(base + r + j + shift, SUBLANES, stride=s), :]
                    acc = tap * taps_w[j] if acc is None else acc + tap * taps_w[j]
                conv_ref[cb, pl.ds(base + r, SUBLANES, stride=s), :] = acc + bias
            base += SUBLANES * s
        return carry

    lax.fori_loop(0, nslab, col_body, 0)

    rows_per_group = 2 * SUBLANES

    def row_body(gi, carry):
        rows = pl.ds(pl.multiple_of(gi * rows_per_group, rows_per_group), rows_per_group)
        u = conv_ref[:, rows, :]
        mu = jnp.sum(jnp.sum(u, axis=0), axis=-1, keepdims=True) / d
        uc = u - mu
        var = jnp.sum(jnp.sum(uc * uc, axis=0), axis=-1, keepdims=True) / d
        y = uc * lax.rsqrt(var + EPS) * g_ref[...] + beta_ref[...]
        y = y * jax.nn.sigmoid(y)
        o_ref[rows, :] = jnp.concatenate([y[c] for c in range(nslab)], axis=-1).astype(o_ref.dtype)
        return carry

    ngroups = tt // rows_per_group
    lax.fori_loop(0, ngroups, row_body, 0, unroll=4 if ngroups % 4 == 0 else 1)


def _conv_ln_swish(um, ux, w_dw, b_dw, ln_g, ln_b, *, seq, meta_mode):
    cur = ux if meta_mode else um
    m, d = cur.shape
    width = w_dw.shape[0]
    nslab = d // LANES
    tt = N_META if meta_mode else min(256, seq)
    tiles_per_seq = seq // tt
    hb = tt // HALO
    nblk = um.shape[0] // HALO
    if meta_mode:
        prev_map = lambda i: (0, 0)
        next_map = lambda i: (i * (seq // HALO), 0)
        meta_map = lambda i: (i, 0)
    else:
        prev_map = lambda i: (jnp.maximum(i * hb - 1, 0), 0)
        next_map = lambda i: (jnp.minimum((i + 1) * hb, nblk - 1), 0)
        meta_map = lambda i: (i // tiles_per_seq, 0)
    slabs = lambda v: v.astype(F32).reshape(1, nslab, LANES).transpose(1, 0, 2)
    w3 = jnp.pad(w_dw.astype(F32), ((0, 2 * HALO - width), (0, 0)))
    w3 = w3.reshape(2 * HALO, nslab, LANES).transpose(1, 0, 2)
    full3 = lambda a: pl.BlockSpec(a.shape, lambda i: (0, 0, 0))
    consts = [w3, slabs(b_dw), slabs(ln_g), slabs(ln_b)]
    return pl.pallas_call(
        functools.partial(_conv_kernel, width=width, tiles_per_seq=tiles_per_seq, meta_mode=meta_mode),
        grid=(m // tt,),
        in_specs=[pl.BlockSpec((tt, d), lambda i: (i, 0)),
                  pl.BlockSpec((HALO, d), prev_map),
                  pl.BlockSpec((HALO, d), next_map),
                  pl.BlockSpec((N_META, d), meta_map)] + [full3(a) for a in consts],
        out_specs=pl.BlockSpec((tt, d), lambda i: (i, 0)),
        out_shape=jax.ShapeDtypeStruct((m, d), BF16),
        scratch_shapes=[pltpu.VMEM((tt + 2 * HALO, LANES), F32), pltpu.VMEM((nslab, tt, LANES), F32)],
        compiler_params=_cparams(("parallel",)),
        name="conv_ln_swish",
    )(cur, um, um, ux, *consts)


def _dot_nt(a, b):
    return lax.dot_general(a, b, (((1,), (1,)), ((), ())), preferred_element_type=F32)


def _dot_tn(a, b):
    return lax.dot_general(a, b, (((0,), (0,)), ((), ())), preferred_element_type=F32)


def _padded_rows(x, r0, total):
    parts = []
    if r0:
        parts.append(jnp.zeros((r0, x.shape[1]), x.dtype))
    parts.append(x)
    tail = total - r0 - x.shape[0]
    if tail:
        parts.append(jnp.zeros((tail, x.shape[1]), x.dtype))
    return parts[0] if len(parts) == 1 else jnp.concatenate(parts, axis=0)


def _gla_step(q, k, v, g, st_ref, *, rev):
    r, dk = q.shape
    c, sb = GLA_CHUNK, GLA_SUB
    nb = r // c
    row = lax.broadcasted_iota(jnp.int32, (r, r), 0)
    col = lax.broadcasted_iota(jnp.int32, (r, r), 1)
    in_chunk = (row // c) == (col // c)
    tri = (in_chunk & ((col >= row) if rev else (col <= row))).astype(BF16)
    g_hi = g.astype(BF16)
    r1 = g - g_hi.astype(F32)
    g_mid = r1.astype(BF16)
    g_lo = (r1 - g_mid.astype(F32)).astype(BF16)
    lc = (jnp.dot(tri, g_hi, preferred_element_type=F32)
          + jnp.dot(tri, g_mid, preferred_element_type=F32)
          + jnp.dot(tri, g_lo, preferred_element_type=F32)) * LOG2_E

    chunks = [slice(b * c, (b + 1) * c) for b in range(nb)]
    tot = [lc[b * c:b * c + 1] if rev else lc[(b + 1) * c - 1:(b + 1) * c] for b in range(nb)]
    before = [None] * nb
    run = jnp.zeros((1, dk), F32)
    for b in (reversed(range(nb)) if rev else range(nb)):
        before[b] = run
        run = run + tot[b]
    total = run

    def earlier(j, i):
        return j > i if rev else j < i

    qn = q * jnp.exp2(lc)
    kn = [k[chunks[b]] * jnp.exp2(tot[b] - lc[chunks[b]]) for b in range(nb)]

    st = st_ref[...]
    qe = jnp.concatenate([qn[chunks[b]] * jnp.exp2(before[b]) for b in range(nb)], axis=0)
    o = _dot_nt(qe.astype(BF16), st.astype(BF16))

    rowl = lax.broadcasted_iota(jnp.int32, (SUBLANES, r), 0)
    coll = lax.broadcasted_iota(jnp.int32, (SUBLANES, r), 1)
    halves = sb // SUBLANES
    score_rows = []
    for b in range(nb):
        qn_b = qn[chunks[b]].astype(BF16)
        parts = [(kn[j] * jnp.exp2(before[b] - before[j] - tot[j])).astype(BF16) if earlier(j, b)
                 else jnp.zeros((c, dk), BF16) for j in range(nb)]
        cross = _dot_nt(qn_b, jnp.concatenate(parts, axis=0)) if nb > 1 else None
        q_c, k_c, lc_c = q[chunks[b]], k[chunks[b]], lc[chunks[b]]
        nsb = c // sb
        for blk in range(nsb):
            r0 = blk * sb
            q_b, lc_b = q_c[r0:r0 + sb], lc_c[r0:r0 + sb]
            a = None if cross is None else cross[r0:r0 + sb]
            if (blk < nsb - 1) if rev else (blk > 0):
                if rev:
                    ref = lc_c[r0 + sb:r0 + sb + 1]
                    k_o, lc_o, at = k_c[r0 + sb:], lc_c[r0 + sb:], b * c + r0 + sb
                else:
                    ref = lc_c[r0 - 1:r0]
                    k_o, lc_o, at = k_c[:r0], lc_c[:r0], b * c
                qs = (q_b * jnp.exp2(lc_b - ref)).astype(BF16)
                ks = (k_o * jnp.exp2(ref - lc_o)).astype(BF16)
                near = _dot_nt(qs, _padded_rows(ks, at, r))
                a = near if a is None else a + near
            if a is None:
                a = jnp.zeros((sb, r), F32)
            a_h = [a[h * SUBLANES:(h + 1) * SUBLANES] for h in range(halves)]
            for jj in range(sb):
                j = r0 + jj
                for h in range(halves):
                    lo = h * SUBLANES
                    if (lo > jj) if rev else (lo + SUBLANES - 1 < jj):
                        continue
                    dec = jnp.exp2(lc_b[lo:lo + SUBLANES] - lc_c[j:j + 1])
                    s = jnp.sum(q_b[lo:lo + SUBLANES] * dec * k_c[j:j + 1], axis=-1, keepdims=True)
                    seen = (rowl + lo <= jj) if rev else (rowl + lo >= jj)
                    a_h[h] = jnp.where((coll == b * c + j) & seen, s, a_h[h])
            score_rows += a_h
    scores = jnp.concatenate(score_rows, axis=0).astype(BF16)
    o = o + jnp.dot(scores, v, preferred_element_type=F32)

    ke = jnp.concatenate([kn[b] * jnp.exp2(total - before[b] - tot[b]) for b in range(nb)], axis=0)
    st_ref[...] = st * jnp.exp2(total) + _dot_tn(v, ke.astype(BF16))
    return o


def _gla_kernel(*refs, rev, nc, final, scale, dk, dv):
    (qm, km, vm, lrm, qx, kx, vx, lrx, wb_ref, bg_ref) = refs[:10]
    pos = 10
    if final:
        obm, obx, ogm, ogx, gn_ref = refs[pos:pos + 5]
        pos += 5
    om, ox, st_ref = refs[pos], refs[pos + 1], refs[pos + 2]

    s = pl.program_id(2)
    chunk = (nc - 1 - s) if rev else s
    is_meta = chunk == 0
    c = qm.shape[0]

    @pl.when(s == 0)
    def _():
        st_ref[...] = jnp.zeros_like(st_ref)

    def pick(x_ref, m_ref, cols=slice(None)):
        return jnp.where(is_meta, x_ref[:, cols], m_ref[:, cols])

    lr = pick(lrx, lrm)
    rows = lax.broadcasted_iota(jnp.int32, (c, dk), 0)
    pad_row = is_meta & (rows < c - N_META)
    outs = []
    for hh in range(st_ref.shape[0]):
        kcols = slice(hh * dk, (hh + 1) * dk)
        vcols = slice(hh * dv, (hh + 1) * dv)
        q = pick(qx, qm, kcols).astype(F32) * scale
        k = pick(kx, km, kcols).astype(F32)
        v = pick(vx, vm, vcols)
        pre = jnp.dot(lr, wb_ref[:, kcols], preferred_element_type=F32) + bg_ref[:, kcols]
        g = (jnp.minimum(pre, 0.0) - jnp.log(1.0 + jnp.exp(-jnp.abs(pre)))) / GLA_GATE_TAU
        g = jnp.where(pad_row, 0.0, g)

        o = _gla_step(q, k, v, g, st_ref.at[hh], rev=rev)

        if final:
            o = o + pick(obx, obm, vcols)
            ms = jnp.mean(o * o, axis=-1, keepdims=True)
            og = pick(ogx, ogm, vcols).astype(F32)
            o = o * lax.rsqrt(ms + EPS) * gn_ref[:, vcols] * (og * jax.nn.sigmoid(og))
        outs.append(o)
    o_all = jnp.concatenate(outs, axis=-1)

    @pl.when(is_meta)
    def _():
        ox[...] = o_all.astype(ox.dtype)

    @pl.when(jnp.logical_not(is_meta))
    def _():
        om[...] = o_all.astype(om.dtype)


def _gla_pass(proj_m, proj_x, lr_m, lr_x, wb_pad, b_gate, *, batch, rev, final=None):
    rows, width = proj_m.shape
    key_dim = wb_pad.shape[1]
    h = GLA_HEADS
    hb = GLA_HEADS_PER_STEP
    dk = key_dim // h
    dv = (width - 2 * key_dim) // 2 // h
    c = GLA_STEP
    nmain = rows // batch // c
    nc = nmain + 1
    ng = h // hb
    wk, wv = hb * dk, hb * dv
    v_col = 2 * key_dim // wv
    g_col = (2 * key_dim + h * dv) // wv
    assert v_col * wv == 2 * key_dim

    def chunk_of(s):
        return (nc - 1 - s) if rev else s

    def main_row(b, s):
        return b * nmain + jnp.maximum(chunk_of(s) - 1, 0)

    def main_spec(w, col0):
        return pl.BlockSpec((c, w), lambda b, hg, s: (main_row(b, s), col0 + hg))

    def meta_spec(w, col0):
        return pl.BlockSpec((c, w), lambda b, hg, s: (b, col0 + hg))

    in_specs = [main_spec(wk, 0), main_spec(wk, ng), main_spec(wv, v_col),
                pl.BlockSpec((c, LANES), lambda b, hg, s: (main_row(b, s), 0)),
                meta_spec(wk, 0), meta_spec(wk, ng), meta_spec(wv, v_col),
                pl.BlockSpec((c, LANES), lambda b, hg, s: (b, 0)),
                pl.BlockSpec((LANES, wk), lambda b, hg, s: (0, hg)),
                pl.BlockSpec((1, wk), lambda b, hg, s: (0, hg))]
    args = [proj_m, proj_m, proj_m, lr_m, proj_x, proj_x, proj_x, lr_x,
            wb_pad, b_gate.reshape(1, key_dim).astype(F32)]
    out_dtype = F32
    if final is not None:
        ob_m, ob_x, g_norm = final
        in_specs += [main_spec(wv, 0), meta_spec(wv, 0), main_spec(wv, g_col), meta_spec(wv, g_col),
                     pl.BlockSpec((1, wv), lambda b, hg, s: (0, hg))]
        args += [ob_m, ob_x, proj_m, proj_x, g_norm.reshape(1, h * dv).astype(F32)]
        out_dtype = BF16
    return pl.pallas_call(
        functools.partial(_gla_kernel, rev=rev, nc=nc, final=final is not None, scale=dk ** -0.5,
                          dk=dk, dv=dv),
        grid=(batch, ng, nc),
        in_specs=in_specs,
        out_specs=[main_spec(wv, 0), meta_spec(wv, 0)],
        out_shape=[jax.ShapeDtypeStruct((rows, h * dv), out_dtype),
                   jax.ShapeDtypeStruct((batch * c, h * dv), out_dtype)],
        scratch_shapes=[pltpu.VMEM((hb, dv, dk), F32)],
        compiler_params=_cparams(("parallel", "parallel", "arbitrary")),
        name="gla_rev" if rev else "gla_fwd",
    )(*args)


def _pad_meta_rows(x, batch):
    w = x.shape[1]
    x = x.reshape(batch, N_META, w)
    x = jnp.pad(x, ((0, 0), (GLA_STEP - N_META, 0), (0, 0)))
    return x.reshape(batch * GLA_STEP, w)


def _residual(a, w, s, emit=True, **kw):
    return _matmul(a, w, kind="residual", residual=s[0], out_dtype=F32, emit_norm_inputs=emit, **kw)


def _mlp(s, p, i, emit):
    u = _matmul(s[1], p["mlp_w1_normed"], layer=i, kind="relu2", row_ss=s[2])
    return _residual(u, p["mlp_w2"], s, emit, layer=i)


def _gla_inputs(s, p):
    proj = _matmul(s[1], p["gla_w_in_normed"], row_ss=s[2])
    lr = _matmul(s[1], p["gla_wa_normed"], row_ss=s[2], out_dtype=F32)
    return proj, lr


def _front(x, ux_one, p):
    batch, seq, d = x.shape
    hm = x.reshape(batch * seq, d)
    zm = _rmsnorm(hm, p["norm_mix"][0], BF16)
    um = _matmul_glu(zm, p["conv_w_pw1"], p["conv_b_pw1"])
    ux = jnp.tile(ux_one, (batch, 1))
    conv_args = (p["conv_w_dw"], p["conv_b_dw"], p["conv_ln_g"], p["conv_ln_b"])
    cm = _conv_ln_swish(um, ux, *conv_args, seq=seq, meta_mode=False)
    cx = _conv_ln_swish(um, ux, *conv_args, seq=seq, meta_mode=True)
    sm = _residual(cm, p["conv_w_pw2"], (hm,), bias=p["conv_b_pw2"])
    sm = _mlp(sm, p, 0, True)
    return sm, _gla_inputs(sm, p), cx


def _back(sm, gla_m, gla_x, p, shape):
    batch, seq, d = shape
    proj_m, lr_m = gla_m
    proj_x, lr_x = (_pad_meta_rows(a, batch) for a in gla_x)
    ob_m, ob_x = _gla_pass(proj_m, proj_x, lr_m, lr_x, p["gla_wb"][1], p["gla_b_gate"][1],
                           batch=batch, rev=True)
    gm, _ = _gla_pass(proj_m, proj_x, lr_m, lr_x, p["gla_wb"][0], p["gla_b_gate"][0],
                      batch=batch, rev=False, final=(ob_m, ob_x, p["gla_norm_g"]))
    sm = _residual(gm, p["gla_w_out"], sm)
    hm = _mlp(sm, p, 1, False)
    return _rmsnorm(hm, p["norm_final"], F32).reshape(batch, seq, d)


def _encode(xs, p):
    d = xs[0].shape[-1]
    meta = p["meta_tokens"].astype(F32)
    ux_one = _matmul_glu(_rmsnorm(meta, p["norm_mix"][0], BF16), p["conv_w_pw1"], p["conv_b_pw1"])
    fronts = [_front(x, ux_one, p) for x in xs]
    nseq = sum(x.shape[0] for x in xs)
    hx = jnp.tile(meta, (nseq, 1))
    cx = jnp.concatenate([f[2] for f in fronts], axis=0)
    sx = _residual(cx, p["conv_w_pw2"], (hx,), bias=p["conv_b_pw2"])
    sx = _mlp(sx, p, 0, True)
    proj_x, lr_x = _gla_inputs(sx, p)
    outs, row = [], 0
    for x, (sm, gla_m, _) in zip(xs, fronts):
        rows = slice(row, row + x.shape[0] * N_META)
        row = rows.stop
        outs.append(_back(sm, gla_m, (proj_x[rows], lr_x[rows]), p, x.shape))
    return tuple(outs)


def kernel(x_prompt, x_sample, meta_tokens, norm_mix, norm_mlp, norm_final, conv_w_pw1, conv_b_pw1, conv_w_dw, conv_b_dw, conv_ln_g, conv_ln_b, conv_w_pw2, conv_b_pw2, gla_w_in, gla_w_gate_a, gla_w_gate_b, gla_b_gate, gla_norm_g, gla_w_out, mlp_w1, mlp_w2):
    assert norm_mix.shape[0] == 2 and conv_w_pw1.shape[0] == 1 and gla_w_in.shape[0] == 1
    d = x_prompt.shape[-1]
    rank = gla_w_gate_a.shape[-1]
    wa = jnp.concatenate([gla_w_gate_a[0, 0], gla_w_gate_a[0, 1],
                          jnp.zeros((d, LANES - 2 * rank), F32)], axis=1)
    wb = jnp.stack([
        jnp.pad(gla_w_gate_b[0, z], ((z * rank, LANES - (z + 1) * rank), (0, 0))) for z in range(2)])

    def normed(gain, w):
        return (gain[:, None] * w).astype(BF16)

    p = dict(
        meta_tokens=meta_tokens, norm_mix=norm_mix, norm_final=norm_final,
        conv_w_pw1=conv_w_pw1[0].astype(BF16), conv_b_pw1=conv_b_pw1[0],
        conv_w_dw=conv_w_dw[0], conv_b_dw=conv_b_dw[0], conv_ln_g=conv_ln_g[0], conv_ln_b=conv_ln_b[0],
        conv_w_pw2=conv_w_pw2[0].astype(BF16), conv_b_pw2=conv_b_pw2[0],
        gla_w_in_normed=normed(norm_mix[1], gla_w_in[0]), gla_wa_normed=normed(norm_mix[1], wa),
        gla_wb=wb, gla_b_gate=gla_b_gate[0],
        gla_norm_g=gla_norm_g[0], gla_w_out=gla_w_out[0].astype(BF16),
        mlp_w1_normed=(norm_mlp[:, :, None] * mlp_w1).astype(BF16), mlp_w2=mlp_w2.astype(BF16),
    )
    return _encode((x_prompt, x_sample), p)
```

```python
import functools

import jax
import jax.numpy as jnp
from jax import lax
from jax.experimental import pallas as pl
from jax.experimental.pallas import tpu as pltpu

N_META = 16
GLA_HEADS = 4
GLA_HEADS_PER_STEP = 1
GLA_GATE_TAU = 16.0
GLA_STEP = 256
GLA_CHUNK = 64
GLA_SUB = 16
EPS = 1e-6
LOG2_E = 1.4426950408889634
HALO = 16
SUBLANES = 8
LANES = 128
VMEM_LIMIT = 52 * 1024 * 1024
MAX_FULL_K = 4096

F32 = jnp.float32
BF16 = jnp.bfloat16


def _cparams(sem):
    return pltpu.CompilerParams(dimension_semantics=sem, vmem_limit_bytes=VMEM_LIMIT)


def _rmsnorm_kernel(x_ref, g_ref, o_ref):
    x = x_ref[...].astype(F32)
    ms = jnp.mean(x * x, axis=-1, keepdims=True)
    o_ref[...] = (x * lax.rsqrt(ms + EPS) * g_ref[...]).astype(o_ref.dtype)


def _rmsnorm(x, g, out_dtype):
    m, d = x.shape
    tr = min(256, m)
    return pl.pallas_call(
        _rmsnorm_kernel,
        grid=(m // tr,),
        in_specs=[pl.BlockSpec((tr, d), lambda i: (i, 0)),
                  pl.BlockSpec((1, d), lambda i: (0, 0))],
        out_specs=pl.BlockSpec((tr, d), lambda i: (i, 0)),
        out_shape=jax.ShapeDtypeStruct((m, d), out_dtype),
        compiler_params=_cparams(("parallel",)),
        name="rmsnorm",
    )(x, g.reshape(1, d).astype(F32))


def _mm_tiles(m, n, k, wide):
    if k > MAX_FULL_K:
        return min(1024, m), min(1024, n), MAX_FULL_K // 2
    return min(1024, m), min(1024 if wide else 512, n), k


def _mm_kernel(*refs, nk, kind, has_bias, has_ss, emit_norm_inputs):
    x_ref, w_ref = refs[0], refs[1]
    pos = 2
    b_ref = r_ref = ss_ref = None
    if has_bias:
        b_ref = refs[pos]
        pos += 1
    if kind == "residual":
        r_ref = refs[pos]
        pos += 1
    if has_ss:
        ss_ref = refs[pos]
        pos += 1
    o_ref = refs[pos]

    def prod():
        return jnp.dot(x_ref[...], w_ref[...], preferred_element_type=F32)

    if kind == "residual":
        def base():
            return r_ref[...] + b_ref[...] if has_bias else r_ref[...]

        def emit(y):
            hb_ref, sso_ref = refs[pos + 1], refs[pos + 2]
            hb_ref[...] = y.astype(hb_ref.dtype)
            piece = jnp.broadcast_to(jnp.sum(y * y, axis=-1, keepdims=True), sso_ref.shape)

            @pl.when(pl.program_id(1) == 0)
            def _():
                sso_ref[...] = piece

            @pl.when(pl.program_id(1) > 0)
            def _():
                sso_ref[...] += piece

        if nk == 1:
            y = base() + prod()
            o_ref[...] = y
            if emit_norm_inputs:
                emit(y)
        else:
            @pl.when(pl.program_id(2) == 0)
            def _():
                o_ref[...] = base()

            if emit_norm_inputs:
                @pl.when(pl.program_id(2) < nk - 1)
                def _():
                    o_ref[...] += prod()

                @pl.when(pl.program_id(2) == nk - 1)
                def _():
                    y = o_ref[...] + prod()
                    o_ref[...] = y
                    emit(y)
            else:
                o_ref[...] += prod()
        return

    assert nk == 1
    y = prod()
    if has_ss:
        ms = ss_ref[:, 0:1] / x_ref.shape[1]
        y = y * lax.rsqrt(ms + EPS)
    if has_bias:
        y = y + b_ref[...]
    if kind == "relu2":
        y = jnp.square(jnp.maximum(y, 0.0))
    o_ref[...] = y.astype(o_ref.dtype)


def _matmul(x, w, *, layer=None, kind="plain", bias=None, residual=None, row_ss=None,
            emit_norm_inputs=False, out_dtype=BF16):
    m, kdim = x.shape
    n = w.shape[-1]
    tm, tn, tk = _mm_tiles(m, n, kdim, wide=kind != "residual")
    nk = kdim // tk
    if layer is None:
        w_spec = pl.BlockSpec((tk, tn), lambda i, j, k: (k, j))
    else:
        w_spec = pl.BlockSpec((None, tk, tn), lambda i, j, k: (layer, k, j))
    in_specs = [pl.BlockSpec((tm, tk), lambda i, j, k: (i, k)), w_spec]
    args = [x, w]
    if bias is not None:
        in_specs.append(pl.BlockSpec((1, tn), lambda i, j, k: (0, j)))
        args.append(bias.reshape(1, n).astype(F32))
    if kind == "residual":
        in_specs.append(pl.BlockSpec((tm, tn), lambda i, j, k: (i, j)))
        args.append(residual)
    if row_ss is not None:
        in_specs.append(pl.BlockSpec((tm, row_ss.shape[1]), lambda i, j, k: (i, 0)))
        args.append(row_ss)
    tile = pl.BlockSpec((tm, tn), lambda i, j, k: (i, j))
    out_specs, out_shape = tile, jax.ShapeDtypeStruct((m, n), out_dtype)
    if emit_norm_inputs:
        out_specs = [tile, tile, pl.BlockSpec((tm, LANES), lambda i, j, k: (i, 0))]
        out_shape = [out_shape, jax.ShapeDtypeStruct((m, n), BF16), jax.ShapeDtypeStruct((m, LANES), F32)]
    return pl.pallas_call(
        functools.partial(_mm_kernel, nk=nk, kind=kind, has_bias=bias is not None,
                          has_ss=row_ss is not None, emit_norm_inputs=emit_norm_inputs),
        grid=(m // tm, n // tn, nk),
        in_specs=in_specs,
        out_specs=out_specs,
        out_shape=out_shape,
        compiler_params=_cparams(("parallel", "arbitrary" if emit_norm_inputs else "parallel", "arbitrary")),
        name="matmul_" + kind,
    )(*args)


def _glu_kernel(x_ref, wa_ref, wg_ref, ba_ref, bg_ref, o_ref):
    x = x_ref[...]
    a = jnp.dot(x, wa_ref[...], preferred_element_type=F32) + ba_ref[...]
    g = jnp.dot(x, wg_ref[...], preferred_element_type=F32) + bg_ref[...]
    o_ref[...] = (a * jax.nn.sigmoid(g)).astype(o_ref.dtype)


def _matmul_glu(x, w, bias):
    m, kdim = x.shape
    n = w.shape[1] // 2
    tm, tn, tk = _mm_tiles(m, n, kdim, wide=False)
    assert tk == kdim
    nj = n // tn
    b = bias.reshape(1, 2 * n).astype(F32)
    return pl.pallas_call(
        _glu_kernel,
        grid=(m // tm, nj),
        in_specs=[pl.BlockSpec((tm, tk), lambda i, j: (i, 0)),
                  pl.BlockSpec((tk, tn), lambda i, j: (0, j)),
                  pl.BlockSpec((tk, tn), lambda i, j: (0, j + nj)),
                  pl.BlockSpec((1, tn), lambda i, j: (0, j)),
                  pl.BlockSpec((1, tn), lambda i, j: (0, j + nj))],
        out_specs=pl.BlockSpec((tm, tn), lambda i, j: (i, j)),
        out_shape=jax.ShapeDtypeStruct((m, n), BF16),
        compiler_params=_cparams(("parallel", "parallel")),
        name="matmul_glu",
    )(x, w, w, b, b)


def _row_strides(tt):
    n = tt // SUBLANES
    if n % SUBLANES:
        return [n]
    a = n // 2 + 1
    return [a, n - a]


def _conv_kernel(cur_ref, prev_ref, next_ref, meta_ref, w_ref, b_ref, g_ref, beta_ref, o_ref,
                 ext_ref, conv_ref, *, width, tiles_per_seq, meta_mode):
    tt, d = cur_ref.shape
    nslab = d // LANES
    i = pl.program_id(0)
    shift = HALO - width // 2
    strides = _row_strides(tt)
    if not meta_mode:
        first = i % tiles_per_seq == 0
        last = i % tiles_per_seq == tiles_per_seq - 1

    def col_body(cb, carry):
        lanes = pl.ds(pl.multiple_of(cb * LANES, LANES), LANES)
        if meta_mode:
            before = jnp.zeros((HALO, LANES), F32)
            after = next_ref[:, lanes].astype(F32)
        else:
            before = jnp.where(first, meta_ref[:, lanes], prev_ref[:, lanes]).astype(F32)
            after = jnp.where(last, 0.0, next_ref[:, lanes].astype(F32))
        ext_ref[0:HALO, :] = before
        ext_ref[HALO:HALO + tt, :] = cur_ref[:, lanes].astype(F32)
        ext_ref[HALO + tt:HALO + tt + HALO, :] = after
        taps_w = [jnp.broadcast_to(w_ref[cb, j:j + 1, :], (SUBLANES, LANES)) for j in range(width)]
        bias = b_ref[cb]
        base = 0
        for s in strides:
            for r in range(s):
                acc = None
                for j in range(width):
                    tap = ext_ref[pl.ds(base + r + j + shift, SUBLANES, stride=s), :]
                    acc = tap * taps_w[j] if acc is None else acc + tap * taps_w[j]
                conv_ref[cb, pl.ds(base + r, SUBLANES, stride=s), :] = acc + bias
            base += SUBLANES * s
        return carry

    lax.fori_loop(0, nslab, col_body, 0)

    rows_per_group = 2 * SUBLANES

    def row_body(gi, carry):
        rows = pl.ds(pl.multiple_of(gi * rows_per_group, rows_per_group), rows_per_group)
        u = conv_ref[:, rows, :]
        mu = jnp.sum(jnp.sum(u, axis=0), axis=-1, keepdims=True) / d
        uc = u - mu
        var = jnp.sum(jnp.sum(uc * uc, axis=0), axis=-1, keepdims=True) / d
        y = uc * lax.rsqrt(var + EPS) * g_ref[...] + beta_ref[...]
        y = y * jax.nn.sigmoid(y)
        o_ref[rows, :] = jnp.concatenate([y[c] for c in range(nslab)], axis=-1).astype(o_ref.dtype)
        return carry

    ngroups = tt // rows_per_group
    lax.fori_loop(0, ngroups, row_body, 0, unroll=4 if ngroups % 4 == 0 else 1)


def _conv_ln_swish(um, ux, w_dw, b_dw, ln_g, ln_b, *, seq, meta_mode):
    cur = ux if meta_mode else um
    m, d = cur.shape
    width = w_dw.shape[0]
    nslab = d // LANES
    tt = N_META if meta_mode else min(256, seq)
    tiles_per_seq = seq // tt
    hb = tt // HALO
    nblk = um.shape[0] // HALO
    if meta_mode:
        prev_map = lambda i: (0, 0)
        next_map = lambda i: (i * (seq // HALO), 0)
        meta_map = lambda i: (i, 0)
    else:
        prev_map = lambda i: (jnp.maximum(i * hb - 1, 0), 0)
        next_map = lambda i: (jnp.minimum((i + 1) * hb, nblk - 1), 0)
        meta_map = lambda i: (i // tiles_per_seq, 0)
    slabs = lambda v: v.astype(F32).reshape(1, nslab, LANES).transpose(1, 0, 2)
    w3 = jnp.pad(w_dw.astype(F32), ((0, 2 * HALO - width), (0, 0)))
    w3 = w3.reshape(2 * HALO, nslab, LANES).transpose(1, 0, 2)
    full3 = lambda a: pl.BlockSpec(a.shape, lambda i: (0, 0, 0))
    consts = [w3, slabs(b_dw), slabs(ln_g), slabs(ln_b)]
    return pl.pallas_call(
        functools.partial(_conv_kernel, width=width, tiles_per_seq=tiles_per_seq, meta_mode=meta_mode),
        grid=(m // tt,),
        in_specs=[pl.BlockSpec((tt, d), lambda i: (i, 0)),
                  pl.BlockSpec((HALO, d), prev_map),
                  pl.BlockSpec((HALO, d), next_map),
                  pl.BlockSpec((N_META, d), meta_map)] + [full3(a) for a in consts],
        out_specs=pl.BlockSpec((tt, d), lambda i: (i, 0)),
        out_shape=jax.ShapeDtypeStruct((m, d), BF16),
        scratch_shapes=[pltpu.VMEM((tt + 2 * HALO, LANES), F32), pltpu.VMEM((nslab, tt, LANES), F32)],
        compiler_params=_cparams(("parallel",)),
        name="conv_ln_swish",
    )(cur, um, um, ux, *consts)


def _dot_nt(a, b):
    return lax.dot_general(a, b, (((1,), (1,)), ((), ())), preferred_element_type=F32)


def _dot_tn(a, b):
    return lax.dot_general(a, b, (((0,), (0,)), ((), ())), preferred_element_type=F32)


def _padded_rows(x, r0, total):
    parts = []
    if r0:
        parts.append(jnp.zeros((r0, x.shape[1]), x.dtype))
    parts.append(x)
    tail = total - r0 - x.shape[0]
    if tail:
        parts.append(jnp.zeros((tail, x.shape[1]), x.dtype))
    return parts[0] if len(parts) == 1 else jnp.concatenate(parts, axis=0)


def _chunk_scan_matrix(r, rev):
    row = lax.broadcasted_iota(jnp.int32, (r, r), 0)
    col = lax.broadcasted_iota(jnp.int32, (r, r), 1)
    in_chunk = (row // GLA_CHUNK) == (col // GLA_CHUNK)
    return (in_chunk & ((col >= row) if rev else (col <= row))).astype(BF16)


def _gla_step(q, k, v, g, tri, st_ref, *, rev):
    r, dk = q.shape
    c, sb = GLA_CHUNK, GLA_SUB
    nb = r // c
    g_hi = g.astype(BF16)
    r1 = g - g_hi.astype(F32)
    g_mid = r1.astype(BF16)
    g_lo = (r1 - g_mid.astype(F32)).astype(BF16)
    lc = (jnp.dot(tri, g_hi, preferred_element_type=F32)
          + jnp.dot(tri, g_mid, preferred_element_type=F32)
          + jnp.dot(tri, g_lo, preferred_element_type=F32)) * LOG2_E

    chunks = [slice(b * c, (b + 1) * c) for b in range(nb)]
    tot = [lc[b * c:b * c + 1] if rev else lc[(b + 1) * c - 1:(b + 1) * c] for b in range(nb)]
    before = [None] * nb
    run = jnp.zeros((1, dk), F32)
    for b in (reversed(range(nb)) if rev else range(nb)):
        before[b] = run
        run = run + tot[b]
    total = run

    def earlier(j, i):
        return j > i if rev else j < i

    qn = q * jnp.exp2(lc)
    kn = [k[chunks[b]] * jnp.exp2(tot[b] - lc[chunks[b]]) for b in range(nb)]

    st = st_ref[...]
    qe = jnp.concatenate([qn[chunks[b]] * jnp.exp2(before[b]) for b in range(nb)], axis=0)
    o = _dot_nt(qe.astype(BF16), st.astype(BF16))

    rowl = lax.broadcasted_iota(jnp.int32, (SUBLANES, r), 0)
    coll = lax.broadcasted_iota(jnp.int32, (SUBLANES, r), 1)
    halves = sb // SUBLANES
    score_rows = []
    for b in range(nb):
        qn_b = qn[chunks[b]].astype(BF16)
        parts = [(kn[j] * jnp.exp2(before[b] - before[j] - tot[j])).astype(BF16) if earlier(j, b)
                 else jnp.zeros((c, dk), BF16) for j in range(nb)]
        cross = _dot_nt(qn_b, jnp.concatenate(parts, axis=0)) if nb > 1 else None
        q_c, k_c, lc_c = q[chunks[b]], k[chunks[b]], lc[chunks[b]]
        nsb = c // sb
        for blk in range(nsb):
            r0 = blk * sb
            q_b, lc_b = q_c[r0:r0 + sb], lc_c[r0:r0 + sb]
            a = None if cross is None else cross[r0:r0 + sb]
            if (blk < nsb - 1) if rev else (blk > 0):
                if rev:
                    ref = lc_c[r0 + sb:r0 + sb + 1]
                    k_o, lc_o, at = k_c[r0 + sb:], lc_c[r0 + sb:], b * c + r0 + sb
                else:
                    ref = lc_c[r0 - 1:r0]
                    k_o, lc_o, at = k_c[:r0], lc_c[:r0], b * c
                qs = (q_b * jnp.exp2(lc_b - ref)).astype(BF16)
                ks = (k_o * jnp.exp2(ref - lc_o)).astype(BF16)
                near = _dot_nt(qs, _padded_rows(ks, at, r))
                a = near if a is None else a + near
            if a is None:
                a = jnp.zeros((sb, r), F32)
            a_h = [a[h * SUBLANES:(h + 1) * SUBLANES] for h in range(halves)]
            for jj in range(sb):
                j = r0 + jj
                for h in range(halves):
                    lo = h * SUBLANES
                    if (lo > jj) if rev else (lo + SUBLANES - 1 < jj):
                        continue
                    dec = jnp.exp2(lc_b[lo:lo + SUBLANES] - lc_c[j:j + 1])
                    s = jnp.sum(q_b[lo:lo + SUBLANES] * dec * k_c[j:j + 1], axis=-1, keepdims=True)
                    seen = (rowl + lo <= jj) if rev else (rowl + lo >= jj)
                    a_h[h] = jnp.where((coll == b * c + j) & seen, s, a_h[h])
            score_rows += a_h
    scores = jnp.concatenate(score_rows, axis=0).astype(BF16)
    o = o + jnp.dot(scores, v, preferred_element_type=F32)

    ke = jnp.concatenate([kn[b] * jnp.exp2(total - before[b] - tot[b]) for b in range(nb)], axis=0)
    st_ref[...] = st * jnp.exp2(total) + _dot_tn(v, ke.astype(BF16))
    return o


def _gla_kernel(*refs, rev, nc, final, scale, dk, dv):
    (qm, km, vm, lrm, qx, kx, vx, lrx, wb_ref, bg_ref, tri_ref) = refs[:11]
    pos = 11
    obm = obx = ogm = ogx = gn_ref = None
    if final:
        obm, obx, ogm, ogx, gn_ref = refs[pos:pos + 5]
        pos += 5
    om, ox, st_ref = refs[pos], refs[pos + 1], refs[pos + 2]

    s = pl.program_id(2)
    chunk = (nc - 1 - s) if rev else s
    is_meta = chunk == 0
    c = qm.shape[0]

    @pl.when(s == 0)
    def _():
        st_ref[...] = jnp.zeros_like(st_ref)

    def advance(q_ref, k_ref, v_ref, lr_ref, ob_ref, og_ref, o_ref, padded):
        lr = lr_ref[...]
        outs = []
        for hh in range(st_ref.shape[0]):
            kcols = slice(hh * dk, (hh + 1) * dk)
            vcols = slice(hh * dv, (hh + 1) * dv)
            q = q_ref[:, kcols].astype(F32) * scale
            k = k_ref[:, kcols].astype(F32)
            pre = jnp.dot(lr, wb_ref[:, kcols], preferred_element_type=F32) + bg_ref[:, kcols]
            g = (jnp.minimum(pre, 0.0) - jnp.log(1.0 + jnp.exp(-jnp.abs(pre)))) / GLA_GATE_TAU
            if padded:
                rows = lax.broadcasted_iota(jnp.int32, (c, dk), 0)
                g = jnp.where(rows < c - N_META, 0.0, g)

            o = _gla_step(q, k, v_ref[:, vcols], g, tri_ref[...], st_ref.at[hh], rev=rev)

            if final:
                o = o + ob_ref[:, vcols]
                ms = jnp.mean(o * o, axis=-1, keepdims=True)
                og = og_ref[:, vcols].astype(F32)
                o = o * lax.rsqrt(ms + EPS) * gn_ref[:, vcols] * (og * jax.nn.sigmoid(og))
            outs.append(o)
        o_ref[...] = jnp.concatenate(outs, axis=-1).astype(o_ref.dtype)

    @pl.when(is_meta)
    def _():
        advance(qx, kx, vx, lrx, obx, ogx, ox, True)

    @pl.when(jnp.logical_not(is_meta))
    def _():
        advance(qm, km, vm, lrm, obm, ogm, om, False)


def _gla_pass(proj_m, proj_x, lr_m, lr_x, wb_pad, b_gate, *, batch, rev, final=None):
    rows, width = proj_m.shape
    key_dim = wb_pad.shape[1]
    h = GLA_HEADS
    hb = GLA_HEADS_PER_STEP
    dk = key_dim // h
    dv = (width - 2 * key_dim) // 2 // h
    c = GLA_STEP
    nmain = rows // batch // c
    nc = nmain + 1
    ng = h // hb
    wk, wv = hb * dk, hb * dv
    v_col = 2 * key_dim // wv
    g_col = (2 * key_dim + h * dv) // wv
    assert v_col * wv == 2 * key_dim

    def chunk_of(s):
        return (nc - 1 - s) if rev else s

    def main_row(b, s):
        return b * nmain + jnp.maximum(chunk_of(s) - 1, 0)

    def main_spec(w, col0):
        return pl.BlockSpec((c, w), lambda b, hg, s: (main_row(b, s), col0 + hg))

    def meta_spec(w, col0):
        return pl.BlockSpec((c, w), lambda b, hg, s: (b, col0 + hg))

    in_specs = [main_spec(wk, 0), main_spec(wk, ng), main_spec(wv, v_col),
                pl.BlockSpec((c, LANES), lambda b, hg, s: (main_row(b, s), 0)),
                meta_spec(wk, 0), meta_spec(wk, ng), meta_spec(wv, v_col),
                pl.BlockSpec((c, LANES), lambda b, hg, s: (b, 0)),
                pl.BlockSpec((LANES, wk), lambda b, hg, s: (0, hg)),
                pl.BlockSpec((1, wk), lambda b, hg, s: (0, hg)),
                pl.BlockSpec((c, c), lambda b, hg, s: (0, 0))]
    args = [proj_m, proj_m, proj_m, lr_m, proj_x, proj_x, proj_x, lr_x,
            wb_pad, b_gate.reshape(1, key_dim).astype(F32), _chunk_scan_matrix(c, rev)]
    out_dtype = F32
    if final is not None:
        ob_m, ob_x, g_norm = final
        in_specs += [main_spec(wv, 0), meta_spec(wv, 0), main_spec(wv, g_col), meta_spec(wv, g_col),
                     pl.BlockSpec((1, wv), lambda b, hg, s: (0, hg))]
        args += [ob_m, ob_x, proj_m, proj_x, g_norm.reshape(1, h * dv).astype(F32)]
        out_dtype = BF16
    return pl.pallas_call(
        functools.partial(_gla_kernel, rev=rev, nc=nc, final=final is not None, scale=dk ** -0.5,
                          dk=dk, dv=dv),
        grid=(batch, ng, nc),
        in_specs=in_specs,
        out_specs=[main_spec(wv, 0), meta_spec(wv, 0)],
        out_shape=[jax.ShapeDtypeStruct((rows, h * dv), out_dtype),
                   jax.ShapeDtypeStruct((batch * c, h * dv), out_dtype)],
        scratch_shapes=[pltpu.VMEM((hb, dv, dk), F32)],
        compiler_params=_cparams(("parallel", "parallel", "arbitrary")),
        name="gla_rev" if rev else "gla_fwd",
    )(*args)


def _pad_meta_rows(x, batch):
    w = x.shape[1]
    x = x.reshape(batch, N_META, w)
    x = jnp.pad(x, ((0, 0), (GLA_STEP - N_META, 0), (0, 0)))
    return x.reshape(batch * GLA_STEP, w)


def _residual(a, w, s, emit=True, **kw):
    return _matmul(a, w, kind="residual", residual=s[0], out_dtype=F32, emit_norm_inputs=emit, **kw)


def _mlp(s, p, i, emit):
    u = _matmul(s[1], p["mlp_w1_normed"], layer=i, kind="relu2", row_ss=s[2])
    return _residual(u, p["mlp_w2"], s, emit, layer=i)


def _gla_inputs(s, p):
    proj = _matmul(s[1], p["gla_w_in_normed"], row_ss=s[2])
    lr = _matmul(s[1], p["gla_wa_normed"], row_ss=s[2], out_dtype=F32)
    return proj, lr


def _front(x, ux_one, p):
    batch, seq, d = x.shape
    hm = x.reshape(batch * seq, d)
    zm = _rmsnorm(hm, p["norm_mix"][0], BF16)
    um = _matmul_glu(zm, p["conv_w_pw1"], p["conv_b_pw1"])
    ux = jnp.tile(ux_one, (batch, 1))
    conv_args = (p["conv_w_dw"], p["conv_b_dw"], p["conv_ln_g"], p["conv_ln_b"])
    cm = _conv_ln_swish(um, ux, *conv_args, seq=seq, meta_mode=False)
    cx = _conv_ln_swish(um, ux, *conv_args, seq=seq, meta_mode=True)
    sm = _residual(cm, p["conv_w_pw2"], (hm,), bias=p["conv_b_pw2"])
    sm = _mlp(sm, p, 0, True)
    return sm, _gla_inputs(sm, p), cx


def _back(sm, gla_m, gla_x, p, shape):
    batch, seq, d = shape
    proj_m, lr_m = gla_m
    proj_x, lr_x = (_pad_meta_rows(a, batch) for a in gla_x)
    ob_m, ob_x = _gla_pass(proj_m, proj_x, lr_m, lr_x, p["gla_wb"][1], p["gla_b_gate"][1],
                           batch=batch, rev=True)
    gm, _ = _gla_pass(proj_m, proj_x, lr_m, lr_x, p["gla_wb"][0], p["gla_b_gate"][0],
                      batch=batch, rev=False, final=(ob_m, ob_x, p["gla_norm_g"]))
    sm = _residual(gm, p["gla_w_out"], sm)
    hm = _mlp(sm, p, 1, False)
    return _rmsnorm(hm, p["norm_final"], F32).reshape(batch, seq, d)


def _encode(xs, p):
    d = xs[0].shape[-1]
    meta = p["meta_tokens"].astype(F32)
    ux_one = _matmul_glu(_rmsnorm(meta, p["norm_mix"][0], BF16), p["conv_w_pw1"], p["conv_b_pw1"])
    fronts = [_front(x, ux_one, p) for x in xs]
    nseq = sum(x.shape[0] for x in xs)
    hx = jnp.tile(meta, (nseq, 1))
    cx = jnp.concatenate([f[2] for f in fronts], axis=0)
    sx = _residual(cx, p["conv_w_pw2"], (hx,), bias=p["conv_b_pw2"])
    sx = _mlp(sx, p, 0, True)
    proj_x, lr_x = _gla_inputs(sx, p)
    outs, row = [], 0
    for x, (sm, gla_m, _) in zip(xs, fronts):
        rows = slice(row, row + x.shape[0] * N_META)
        row = rows.stop
        outs.append(_back(sm, gla_m, (proj_x[rows], lr_x[rows]), p, x.shape))
    return tuple(outs)


def kernel(x_prompt, x_sample, meta_tokens, norm_mix, norm_mlp, norm_final, conv_w_pw1, conv_b_pw1, conv_w_dw, conv_b_dw, conv_ln_g, conv_ln_b, conv_w_pw2, conv_b_pw2, gla_w_in, gla_w_gate_a, gla_w_gate_b, gla_b_gate, gla_norm_g, gla_w_out, mlp_w1, mlp_w2):
    assert norm_mix.shape[0] == 2 and conv_w_pw1.shape[0] == 1 and gla_w_in.shape[0] == 1
    d = x_prompt.shape[-1]
    rank = gla_w_gate_a.shape[-1]
    wa = jnp.concatenate([gla_w_gate_a[0, 0], gla_w_gate_a[0, 1],
                          jnp.zeros((d, LANES - 2 * rank), F32)], axis=1)
    wb = jnp.stack([
        jnp.pad(gla_w_gate_b[0, z], ((z * rank, LANES - (z + 1) * rank), (0, 0))) for z in range(2)])

    def normed(gain, w):
        return (gain[:, None] * w).astype(BF16)

    p = dict(
        meta_tokens=meta_tokens, norm_mix=norm_mix, norm_final=norm_final,
        conv_w_pw1=conv_w_pw1[0].astype(BF16), conv_b_pw1=conv_b_pw1[0],
        conv_w_dw=conv_w_dw[0], conv_b_dw=conv_b_dw[0], conv_ln_g=conv_ln_g[0], conv_ln_b=conv_ln_b[0],
        conv_w_pw2=conv_w_pw2[0].astype(BF16), conv_b_pw2=conv_b_pw2[0],
        gla_w_in_normed=normed(norm_mix[1], gla_w_in[0]), gla_wa_normed=normed(norm_mix[1], wa),
        gla_wb=wb, gla_b_gate=gla_b_gate[0],
        gla_norm_g=gla_norm_g[0], gla_w_out=gla_w_out[0].astype(BF16),
        mlp_w1_normed=(norm_mlp[:, :, None] * mlp_w1).astype(BF16), mlp_w2=mlp_w2.astype(BF16),
    )
    return _encode((x_prompt, x_sample), p)
```
